```python
import math
import jax, jax.numpy as jnp
from jax import lax
import numpy as np

D_MODEL = 4096
BATCH = 4
SEQ = 2048
DEPTH = 4
DEC_BATCH = 8
DEC_SEQ = 4
PAST_LEN = 8192
PAGE_SIZE = 128

N_HEADS = 32
HEAD_DIM = D_MODEL // N_HEADS
N_KV = 4
GROUP = N_HEADS // N_KV
CMP_LEN = 32
CMP_STRIDE = 16
CMP_HIDDEN = 2 * HEAD_DIM
SEL_BLOCK = 64
N_SEL = 16
WINDOW = 512
CONV_W = 3
D_CONV = D_MODEL
D_FF = 4 * D_MODEL
N_NSA = (DEPTH + 1) // 2
N_CONV = DEPTH // 2
Q_DIM = N_HEADS * HEAD_DIM
KV_DIM = 2 * N_KV * HEAD_DIM
NSA_IN = Q_DIM + 3 * KV_DIM + 3 * N_HEADS
RMS_EPS = 1e-6
NEG = -1e30
BIG = 1e30
SEL_Q_BLOCK = 32
WIN_Q_BLOCK = 128

kernel_name = 'nsa_shortconv_hybrid_step'


def alibi_slopes():
    i = np.arange(1, N_HEADS + 1, dtype=np.float32)
    return jnp.asarray(2.0 ** (-8.0 * i / N_HEADS), dtype=jnp.float32).reshape(N_KV, GROUP)


def rmsnorm(x, g):
    xf = x.astype(jnp.float32)
    y = xf * lax.rsqrt(jnp.mean(xf * xf, axis=-1, keepdims=True) + RMS_EPS)
    return (y * g.astype(jnp.float32)).astype(x.dtype)


def masked_softmax(s, mask):
    s = jnp.where(mask, s, NEG)
    return jax.nn.softmax(s, axis=-1) * mask.astype(jnp.float32)


def compress(kv, pe, w1, w2):
    B, T = kv.shape[0], kv.shape[1]
    n = (T - CMP_LEN) // CMP_STRIDE + 1
    r = CMP_LEN // CMP_STRIDE
    chunks = kv[:, :(n + r - 1) * CMP_STRIDE].reshape(B, n + r - 1, CMP_STRIDE, 2, N_KV, HEAD_DIM)
    blocks = jnp.concatenate([chunks[:, j:j + n] for j in range(r)], axis=2)
    blocks = blocks + pe.transpose(1, 0, 2)[:, :, None, :]
    hid = jax.nn.gelu(jnp.einsum('bnlcgd,clde->bncge', blocks, w1))
    out = jnp.einsum('bncge,ced->bncgd', hid, w2)
    return out[:, :, 0], out[:, :, 1]


def nsa_project(h, w_in):
    B, T, _ = h.shape
    z = h @ w_in
    q = z[..., :Q_DIM].reshape(B, T, N_KV, GROUP, HEAD_DIM)
    kv = z[..., Q_DIM:Q_DIM + 3 * KV_DIM].reshape(B, T, 3, 2, N_KV, HEAD_DIM)
    gates = jax.nn.sigmoid(z[..., Q_DIM + 3 * KV_DIM:].astype(jnp.float32))
    gates = gates.reshape(B, T, 3, N_KV, GROUP).astype(h.dtype)
    return q, kv[:, :, 0], kv[:, :, 1], kv[:, :, 2], gates


def nsa_attend(q, kv_cmp, kv_slc, kv_win, win_pos0, gates, pe, w1, w2, slopes):
    B, Tq = q.shape[0], q.shape[1]
    T = kv_cmp.shape[1]
    q0 = T - Tq
    scale = HEAD_DIM ** -0.5
    tpos = q0 + jnp.arange(Tq)

    k_c, v_c = compress(kv_cmp, pe, w1, w2)
    n_c = k_c.shape[1]
    cpos = jnp.arange(n_c) * CMP_STRIDE + CMP_LEN - 1
    dist_c = tpos[:, None] - cpos[None, :]
    s_c = (jnp.einsum('btgrd,bngd->bgrtn', q, k_c).astype(jnp.float32) * scale
           - slopes[None, :, :, None, None] * dist_c.astype(jnp.float32))
    p_c = masked_softmax(s_c, dist_c >= 0)
    o_c = jnp.einsum('bgrtn,bngd->btgrd', p_c.astype(v_c.dtype), v_c)

    n_s = -(-T // SEL_BLOCK)
    ci = jnp.arange(n_c)[:, None]
    sj = jnp.arange(n_s)[None, :]
    overlap = ((ci * CMP_STRIDE <= sj * SEL_BLOCK + SEL_BLOCK - 1)
               & (ci * CMP_STRIDE + CMP_LEN - 1 >= sj * SEL_BLOCK)).astype(jnp.float32)
    imp = jnp.einsum('bgrtn,ns->bgts', p_c, overlap)
    blk_t = tpos // SEL_BLOCK
    forced = (sj == 0) | (sj == blk_t[:, None]) | (sj == blk_t[:, None] - 1)
    valid = sj <= blk_t[:, None]
    imp = jnp.where(valid, jnp.where(forced, BIG, imp), NEG)
    k_sel = min(N_SEL, n_s)
    _, idx = lax.top_k(imp, k_sel)

    pad = n_s * SEL_BLOCK - T
    ks = jnp.pad(kv_slc, ((0, 0), (0, pad), (0, 0), (0, 0), (0, 0)))
    ks = ks.reshape(B, n_s, SEL_BLOCK, 2, N_KV, HEAD_DIM).transpose(0, 4, 1, 2, 3, 5)
    qb = math.gcd(Tq, SEL_Q_BLOCK)
    nb = Tq // qb
    q_blocks = q.reshape(B, nb, qb, N_KV, GROUP, HEAD_DIM).transpose(1, 0, 3, 2, 4, 5)
    idx_blocks = idx.reshape(B, N_KV, nb, qb, k_sel).transpose(2, 0, 1, 3, 4)
    pos_blocks = tpos.reshape(nb, qb)
    b_ix = jnp.arange(B)[:, None, None, None]
    g_ix = jnp.arange(N_KV)[None, :, None, None]

    def sel_block(args):
        qq, ii, tp = args
        kvg = ks[b_ix, g_ix, ii]
        kpos = ii[..., None] * SEL_BLOCK + jnp.arange(SEL_BLOCK)
        dist = tp[None, None, :, None, None] - kpos
        ok = (dist >= 0) & (ii <= (tp // SEL_BLOCK)[None, None, :, None])[..., None]
        s = (jnp.einsum('bgtrd,bgtkld->bgtrkl', qq, kvg[..., 0, :]).astype(jnp.float32) * scale
             - slopes[None, :, None, :, None, None] * dist[:, :, :, None].astype(jnp.float32))
        s = s.reshape(B, N_KV, qb, GROUP, k_sel * SEL_BLOCK)
        mask = ok.reshape(B, N_KV, qb, 1, k_sel * SEL_BLOCK)
        p = masked_softmax(s, mask)
        vg = kvg[..., 1, :].reshape(B, N_KV, qb, k_sel * SEL_BLOCK, HEAD_DIM)
        return jnp.einsum('bgtrm,bgtmd->bgtrd', p.astype(vg.dtype), vg)

    o_s = lax.map(sel_block, (q_blocks, idx_blocks, pos_blocks))
    o_s = o_s.transpose(1, 0, 3, 2, 4, 5).reshape(B, Tq, N_KV, GROUP, HEAD_DIM)

    qw = math.gcd(Tq, WIN_Q_BLOCK)
    nbw = Tq // qw
    kvp = jnp.pad(kv_win, ((0, 0), (WINDOW, 0), (0, 0), (0, 0), (0, 0)))
    q_w = q.reshape(B, nbw, qw, N_KV, GROUP, HEAD_DIM).transpose(1, 0, 2, 3, 4, 5)

    def win_block(args):
        qq, i = args
        t0 = q0 + i * qw
        kk = lax.dynamic_slice_in_dim(kvp, t0 - win_pos0, WINDOW + qw, axis=1)
        kpos = t0 - WINDOW + jnp.arange(WINDOW + qw)
        tp = t0 + jnp.arange(qw)
        dist = tp[:, None] - kpos[None, :]
        ok = (dist >= 0) & (dist < WINDOW) & (kpos[None, :] >= win_pos0)
        s = (jnp.einsum('btgrd,bsgd->bgrts', qq, kk[:, :, 0]).astype(jnp.float32) * scale
             - slopes[None, :, :, None, None] * dist.astype(jnp.float32))
        p = masked_softmax(s, ok)
        return jnp.einsum('bgrts,bsgd->btgrd', p.astype(kk.dtype), kk[:, :, 1])

    o_w = lax.map(win_block, (q_w, jnp.arange(nbw)))
    o_w = o_w.transpose(1, 0, 2, 3, 4, 5).reshape(B, Tq, N_KV, GROUP, HEAD_DIM)

    o = (gates[:, :, 0, :, :, None] * o_c + gates[:, :, 1, :, :, None] * o_s
         + gates[:, :, 2, :, :, None] * o_w)
    return o.reshape(B, Tq, Q_DIM)


def nsa_prompt(h, w_in, pe, w1, w2, w_out, slopes, win_buf):
    T = h.shape[1]
    q, kv_c, kv_s, kv_w, gates = nsa_project(h, w_in)
    o = nsa_attend(q, kv_c, kv_s, kv_w, 0, gates, pe, w1, w2, slopes)
    kv_w_pad = jnp.pad(kv_w, ((0, 0), (max(win_buf - T, 0), 0), (0, 0), (0, 0), (0, 0)))
    return o @ w_out, kv_c, kv_s, kv_w_pad[:, kv_w_pad.shape[1] - win_buf:]


def nsa_sample(h, pool_c, pool_s, win_state, page_table, w_in, pe, w1, w2, w_out, slopes):
    B = h.shape[0]
    past = page_table.shape[1] * PAGE_SIZE
    win_buf = win_state.shape[1]
    q, kv_c, kv_s, kv_w, gates = nsa_project(h, w_in)
    kc_full = jnp.concatenate([pool_c[page_table].reshape(B, past, 2, N_KV, HEAD_DIM), kv_c], axis=1)
    ks_full = jnp.concatenate([pool_s[page_table].reshape(B, past, 2, N_KV, HEAD_DIM), kv_s], axis=1)
    kw_full = jnp.concatenate([win_state, kv_w], axis=1)
    o = nsa_attend(q, kc_full, ks_full, kw_full, past - win_buf, gates, pe, w1, w2, slopes)
    return o @ w_out, kv_c, kv_s, kw_full[:, kw_full.shape[1] - win_buf:]


def conv_mix(h, prev, w_in, conv_k, w_out):
    z = h @ w_in
    b_gate, c_gate, u = jnp.split(z, 3, axis=-1)
    v = c_gate * u
    vp = jnp.concatenate([prev.astype(v.dtype), v], axis=1)
    y = lax.conv_general_dilated(vp, conv_k[:, None, :].astype(v.dtype), window_strides=(1,),
                                 padding='VALID', dimension_numbers=('NWC', 'WIO', 'NWC'),
                                 feature_group_count=D_CONV)
    return (b_gate * y) @ w_out, vp[:, vp.shape[1] - (CONV_W - 1):]


def sqrelu_mlp(h, w_up, w_down):
    a = jax.nn.relu(h @ w_up)
    return (a * a) @ w_down


def setup_inputs(seed: int = 0) -> dict:
    key = jax.random.key(seed)
    ks = jax.random.split(key, 24)
    n_pages = PAST_LEN // PAGE_SIZE
    n_used = DEC_BATCH * n_pages
    n_pool = n_used + (n_used + 3) // 4
    win_buf = min(WINDOW, PAST_LEN)
    nrm = jax.random.normal
    f32 = jnp.float32
    page_table = jax.random.permutation(ks[6], n_pool)[:n_used].reshape(DEC_BATCH, n_pages).astype(jnp.int32)
    return {
        'x_prompt': nrm(ks[0], (BATCH, SEQ, D_MODEL), f32),
        'x_sample': nrm(ks[1], (DEC_BATCH, DEC_SEQ, D_MODEL), f32),
        'cache_kv_cmp': nrm(ks[2], (N_NSA, n_pool, PAGE_SIZE, 2, N_KV, HEAD_DIM), f32),
        'cache_kv_slc': nrm(ks[3], (N_NSA, n_pool, PAGE_SIZE, 2, N_KV, HEAD_DIM), f32),
        'state_win_kv': nrm(ks[4], (N_NSA, DEC_BATCH, win_buf, 2, N_KV, HEAD_DIM), f32),
        'state_conv': nrm(ks[5], (N_CONV, DEC_BATCH, CONV_W - 1, D_CONV), f32),
        'page_table': page_table,
        'norm_mix': 1.0 + 0.02 * nrm(ks[7], (DEPTH, D_MODEL), f32),
        'norm_mlp': 1.0 + 0.02 * nrm(ks[8], (DEPTH, D_MODEL), f32),
        'norm_final': 1.0 + 0.02 * nrm(ks[9], (D_MODEL,), f32),
        'nsa_w_in': nrm(ks[10], (N_NSA, D_MODEL, NSA_IN), f32) * D_MODEL ** -0.5,
        'nsa_cmp_pe': 0.1 * nrm(ks[11], (N_NSA, 2, CMP_LEN, HEAD_DIM), f32),
        'nsa_cmp_w1': nrm(ks[12], (N_NSA, 2, CMP_LEN, HEAD_DIM, CMP_HIDDEN), f32) * (CMP_LEN * HEAD_DIM) ** -0.5,
        'nsa_cmp_w2': nrm(ks[13], (N_NSA, 2, CMP_HIDDEN, HEAD_DIM), f32) * CMP_HIDDEN ** -0.5,
        'nsa_w_out': nrm(ks[14], (N_NSA, Q_DIM, D_MODEL), f32) * Q_DIM ** -0.5,
        'conv_w_in': nrm(ks[15], (N_CONV, D_MODEL, 3 * D_CONV), f32) * D_MODEL ** -0.5,
        'conv_kernel': nrm(ks[16], (N_CONV, CONV_W, D_CONV), f32) * CONV_W ** -0.5,
        'conv_w_out': nrm(ks[17], (N_CONV, D_CONV, D_MODEL), f32) * D_CONV ** -0.5,
        'mlp_w_up': nrm(ks[18], (DEPTH, D_MODEL, D_FF), f32) * D_MODEL ** -0.5,
        'mlp_w_down': nrm(ks[19], (DEPTH, D_FF, D_MODEL), f32) * D_FF ** -0.5,
    }


def reference(x_prompt, x_sample, cache_kv_cmp, cache_kv_slc, state_win_kv, state_conv, page_table,
              norm_mix, norm_mlp, norm_final, nsa_w_in, nsa_cmp_pe, nsa_cmp_w1, nsa_cmp_w2, nsa_w_out,
              conv_w_in, conv_kernel, conv_w_out, mlp_w_up, mlp_w_down):
    slopes = alibi_slopes()
    win_buf = state_win_kv.shape[2]
    xp, xs = x_prompt, x_sample
    kvc_p, kvc_s, kvs_p, kvs_s, win_p, win_s, cv_p, cv_s = [], [], [], [], [], [], [], []
    for i in range(DEPTH):
        j = i // 2
        hp = rmsnorm(xp, norm_mix[i])
        hs = rmsnorm(xs, norm_mix[i])
        if i % 2 == 0:
            yp, a_p, b_p, w_p = nsa_prompt(hp, nsa_w_in[j], nsa_cmp_pe[j], nsa_cmp_w1[j], nsa_cmp_w2[j],
                                           nsa_w_out[j], slopes, win_buf)
            ys, a_s, b_s, w_s = nsa_sample(hs, cache_kv_cmp[j], cache_kv_slc[j], state_win_kv[j], page_table,
                                           nsa_w_in[j], nsa_cmp_pe[j], nsa_cmp_w1[j], nsa_cmp_w2[j],
                                           nsa_w_out[j], slopes)
            kvc_p.append(a_p); kvc_s.append(a_s)
            kvs_p.append(b_p); kvs_s.append(b_s)
            win_p.append(w_p); win_s.append(w_s)
        else:
            zeros = jnp.zeros((xp.shape[0], CONV_W - 1, D_CONV), xp.dtype)
            yp, c_p = conv_mix(hp, zeros, conv_w_in[j], conv_kernel[j], conv_w_out[j])
            ys, c_s = conv_mix(hs, state_conv[j], conv_w_in[j], conv_kernel[j], conv_w_out[j])
            cv_p.append(c_p); cv_s.append(c_s)
        xp = xp + yp
        xs = xs + ys
        xp = xp + sqrelu_mlp(rmsnorm(xp, norm_mlp[i]), mlp_w_up[i], mlp_w_down[i])
        xs = xs + sqrelu_mlp(rmsnorm(xs, norm_mlp[i]), mlp_w_up[i], mlp_w_down[i])
    y_prompt = rmsnorm(xp, norm_final)
    y_sample = rmsnorm(xs, norm_final)
    return (y_prompt, y_sample, jnp.stack(kvc_p), jnp.stack(kvc_s), jnp.stack(kvs_p), jnp.stack(kvs_s),
            jnp.stack(win_p), jnp.stack(win_s), jnp.stack(cv_p), jnp.stack(cv_s))
```

```python
import functools

import numpy as np
import jax
import jax.numpy as jnp
from jax import lax
from jax.experimental import pallas as pl
from jax.experimental.pallas import tpu as pltpu

N_HEADS = 32
HEAD_DIM = 128
N_KV = 4
GROUP = N_HEADS // N_KV
CMP_LEN = 32
CMP_STRIDE = 16
SEL_BLOCK = 64
N_SEL = 16
WINDOW = 512
CONV_W = 3
PAGE_SIZE = 128
RMS_EPS = 1e-6
NEG = -1e30
BIG = 1e30
SCALE = HEAD_DIM ** -0.5
Q_DIM = N_HEADS * HEAD_DIM
KV_DIM = 2 * N_KV * HEAD_DIM
GATE_LANES = 128

V7X_VMEM_LIMIT = 56 * 1024 * 1024
BF16 = jnp.bfloat16
F32 = jnp.float32
NT_DIMS = (((1,), (1,)), ((), ()))


def _alibi_slopes():
    i = np.arange(1, N_HEADS + 1, dtype=np.float32)
    return jnp.asarray(2.0 ** (-8.0 * i / N_HEADS), dtype=jnp.float32)


def _dot(a, b):
    return jnp.dot(a, b, preferred_element_type=F32)


def _dot_nt(a, b):
    return lax.dot_general(a, b, NT_DIMS, preferred_element_type=F32)


def _dot_split3(x, m):
    hi = x.astype(BF16)
    r1 = x - hi.astype(F32)
    mid = r1.astype(BF16)
    lo = (r1 - mid.astype(F32)).astype(BF16)
    return _dot(hi, m) + _dot(mid, m) + _dot(lo, m)


def _rms_kernel(x_ref, g_ref, o_ref):
    x = x_ref[...]
    ms = jnp.mean(x * x, axis=-1, keepdims=True)
    o_ref[...] = (x * lax.rsqrt(ms + RMS_EPS) * g_ref[...]).astype(o_ref.dtype)


def _rmsnorm(x, g, out_dtype):
    m, d = x.shape
    tm = min(m, 256)
    return pl.pallas_call(
        _rms_kernel,
        grid=(m // tm,),
        in_specs=[pl.BlockSpec((tm, d), lambda i: (i, 0)),
                  pl.BlockSpec((1, d), lambda i: (0, 0))],
        out_specs=pl.BlockSpec((tm, d), lambda i: (i, 0)),
        out_shape=jax.ShapeDtypeStruct((m, d), out_dtype),
        name="rmsnorm",
    )(x, g.reshape(1, d))


def _gemm_kernel(*refs, nk, act, has_res):
    a_ref, w_ref = refs[0], refs[1]
    res_ref = refs[2] if has_res else None
    o_ref = refs[3] if has_res else refs[2]

    def finish(acc):
        if act == "relu2":
            acc = jnp.maximum(acc, 0.0)
            acc = acc * acc
        if has_res:
            acc = acc + res_ref[...]
        o_ref[...] = acc.astype(o_ref.dtype)

    if nk == 1:
        finish(_dot(a_ref[...], w_ref[...]))
        return

    acc_ref = refs[-1]
    k = pl.program_id(2)
    part = _dot(a_ref[...], w_ref[...])

    @pl.when(k == 0)
    def _():
        acc_ref[...] = part

    @pl.when(k > 0)
    def _():
        acc_ref[...] += part

    @pl.when(k == nk - 1)
    def _():
        finish(acc_ref[...])


def _gemm(a, w, *, res=None, act=None, out_dtype=F32, tm=1024, tn=1024, tk=None):
    m, kdim = a.shape
    n = w.shape[1]
    tm, tn = min(tm, m), min(tn, n)
    tk = kdim if tk is None else min(tk, kdim)
    assert m % tm == 0 and n % tn == 0 and kdim % tk == 0
    nk = kdim // tk
    in_specs = [pl.BlockSpec((tm, tk), lambda i, j, k: (i, k)),
                pl.BlockSpec((tk, tn), lambda i, j, k: (k, j))]
    args = [a, w]
    if res is not None:
        in_specs.append(pl.BlockSpec((tm, tn), lambda i, j, k: (i, j)))
        args.append(res)
    return pl.pallas_call(
        functools.partial(_gemm_kernel, nk=nk, act=act, has_res=res is not None),
        grid=(m // tm, n // tn, nk),
        in_specs=in_specs,
        out_specs=pl.BlockSpec((tm, tn), lambda i, j, k: (i, j)),
        out_shape=jax.ShapeDtypeStruct((m, n), out_dtype),
        scratch_shapes=[pltpu.VMEM((tm, tn), F32)] if nk > 1 else [],
        compiler_params=pltpu.CompilerParams(
            dimension_semantics=("parallel", "parallel", "arbitrary"),
            vmem_limit_bytes=V7X_VMEM_LIMIT),
        name="gemm",
    )(*args)


def _compress_kernel(x_ref, pe_ref, w1_ref, w2_ref, o_ref):
    x = x_ref[...]
    half = x.shape[1]
    rows = x.shape[0]
    lo = _dot((x + pe_ref[0:1, :]).astype(BF16), w1_ref[0:half, :])
    hi = _dot((x + pe_ref[1:2, :]).astype(BF16), w1_ref[half:2 * half, :])
    hid = jax.nn.gelu(lo + pltpu.roll(hi, rows - 1, 0))
    o_ref[...] = _dot(hid.astype(BF16), w2_ref[...])


def _compress_weights(pe, w1, w2):
    half = CMP_STRIDE * HEAD_DIM
    pe2 = pe.reshape(2, CMP_LEN // CMP_STRIDE, half)
    w1f = w1.reshape(2, CMP_LEN * HEAD_DIM, w1.shape[-1]).astype(BF16)
    return pe2, w1f, w2.astype(BF16)


def _compress(x, pe2, w1f, w2b, rows):
    na, _, rt, half = x.shape
    hidden = w1f.shape[-1]
    return pl.pallas_call(
        _compress_kernel,
        grid=(2, na, rt // rows),
        in_specs=[pl.BlockSpec((None, None, rows, half), lambda c, a, i: (a, c, i, 0)),
                  pl.BlockSpec((None, 2, half), lambda c, a, i: (c, 0, 0)),
                  pl.BlockSpec((None, 2 * half, hidden), lambda c, a, i: (c, 0, 0)),
                  pl.BlockSpec((None, hidden, HEAD_DIM), lambda c, a, i: (c, 0, 0))],
        out_specs=pl.BlockSpec((None, None, rows, HEAD_DIM), lambda c, a, i: (c, a, i, 0)),
        out_shape=jax.ShapeDtypeStruct((2, na, rt, HEAD_DIM), F32),
        compiler_params=pltpu.CompilerParams(vmem_limit_bytes=V7X_VMEM_LIMIT),
        name="compress",
    )(x, pe2, w1f, w2b)


def _softmax_rows(s, mask):
    s = jnp.where(mask, s, NEG)
    m = jnp.max(s, axis=-1, keepdims=True)
    e = jnp.exp(s - m)
    return jnp.where(mask, e / jnp.sum(e, axis=-1, keepdims=True), 0.0)


def _select_blocks(imp, blk_t, n_s, k_sel):
    lanes = imp.shape[-1]
    s_idx = lax.broadcasted_iota(jnp.int32, (1, lanes), 1)
    forced = (s_idx == 0) | (s_idx == blk_t) | (s_idx == blk_t - 1)
    imp = jnp.where(s_idx <= blk_t, jnp.where(forced, BIG, imp), NEG)
    imp = jnp.where(s_idx < n_s, imp, -jnp.inf)
    s_f = s_idx.astype(F32)
    sel = jnp.zeros(imp.shape, F32)
    for _ in range(k_sel):
        top = jnp.max(imp, axis=-1, keepdims=True)
        first = jnp.min(jnp.where(imp == top, s_f, float(lanes)), axis=-1, keepdims=True)
        hit = s_f == first
        sel = jnp.where(hit, 1.0, sel)
        imp = jnp.where(hit, -jnp.inf, imp)
    return sel


def _overlap_matrix(n_c, n_s, rows, cols):
    ci = np.arange(rows)[:, None]
    sj = np.arange(cols)[None, :]
    ov = ((ci * CMP_STRIDE <= sj * SEL_BLOCK + SEL_BLOCK - 1)
          & (ci * CMP_STRIDE + CMP_LEN - 1 >= sj * SEL_BLOCK)
          & (ci < n_c) & (sj < n_s))
    return jnp.asarray(ov, dtype=BF16)


def _nsa_prompt_kernel(sl_ref, q_ref, ks_ref, vs_ref, kw_ref, vw_ref, kc_ref, vc_ref, gz_ref, ov_ref,
                       o_ref, qb_sc, oc_sc, m_sc, l_sc, acc_sc, *, seq, tq, n_c, n_s, k_sel, wslab):
    g = pl.program_id(1)
    qi = pl.program_id(2)
    t0 = qi * tq
    tpos = t0 + lax.broadcasted_iota(jnp.int32, (tq, 1), 0)
    qb_sc[...] = q_ref[...].astype(BF16)

    kc = kc_ref[...].astype(BF16)
    vc = vc_ref[...].astype(BF16)
    lanes_c = kc.shape[0]
    n_idx = lax.broadcasted_iota(jnp.int32, (1, lanes_c), 1)
    dist_c = tpos - (n_idx * CMP_STRIDE + (CMP_LEN - 1))
    mask_c = (n_idx < n_c) & (dist_c >= 0)
    dist_cf = dist_c.astype(F32)
    psum = jnp.zeros((tq, lanes_c), F32)
    for r in range(GROUP):
        slope = sl_ref[g * GROUP + r]
        s = _dot_nt(qb_sc[:, r * HEAD_DIM:(r + 1) * HEAD_DIM], kc) * SCALE - slope * dist_cf
        p = _softmax_rows(s, mask_c)
        oc_sc[r] = _dot(p.astype(BF16), vc)
        psum = psum + p
    imp = _dot_split3(psum, ov_ref[...])
    sel = _select_blocks(imp, tpos // SEL_BLOCK, n_s, k_sel).astype(BF16)

    m_sc[...] = jnp.full(m_sc.shape, NEG, F32)
    l_sc[...] = jnp.zeros(l_sc.shape, F32)
    acc_sc[...] = jnp.zeros(acc_sc.shape, F32)
    sel_lanes = sel.shape[1]

    def key_tile(j, carry):
        k0 = pl.multiple_of(j * tq, tq)
        kt = ks_ref[pl.ds(k0, tq), :].astype(BF16)
        vt = vs_ref[pl.ds(k0, tq), :].astype(BF16)
        kpos = k0 + lax.broadcasted_iota(jnp.int32, (1, tq), 1)
        blk_of_key = (k0 + lax.broadcasted_iota(jnp.int32, (sel_lanes, tq), 1)) // SEL_BLOCK
        expand = jnp.where(lax.broadcasted_iota(jnp.int32, (sel_lanes, tq), 0) == blk_of_key, 1.0, 0.0)
        dist = tpos - kpos
        allowed = jnp.where(dist >= 0, _dot(sel, expand.astype(BF16)), 0.0) > 0.5
        dist_f = dist.astype(F32)
        for r in range(GROUP):
            slope = sl_ref[g * GROUP + r]
            s = _dot_nt(qb_sc[:, r * HEAD_DIM:(r + 1) * HEAD_DIM], kt) * SCALE - slope * dist_f
            s = jnp.where(allowed, s, NEG)
            m_old = m_sc[r]
            m_new = jnp.maximum(m_old, jnp.max(s, axis=-1, keepdims=True))
            alpha = jnp.exp(m_old - m_new)
            p = jnp.where(allowed, jnp.exp(s - m_new), 0.0)
            l_sc[r] = alpha * l_sc[r] + jnp.sum(p, axis=-1, keepdims=True)
            acc_sc[r] = alpha * acc_sc[r] + _dot(p.astype(BF16), vt)
            m_sc[r] = m_new
        return carry

    lax.fori_loop(0, qi + 1, key_tile, 0)

    w0 = jnp.clip(t0 - WINDOW, 0, seq - wslab)
    w0 = pl.multiple_of(w0, tq)
    kw = kw_ref[pl.ds(w0, wslab), :].astype(BF16)
    vw = vw_ref[pl.ds(w0, wslab), :].astype(BF16)
    dist_w = tpos - (w0 + lax.broadcasted_iota(jnp.int32, (1, wslab), 1))
    mask_w = (dist_w >= 0) & (dist_w < WINDOW)
    dist_wf = dist_w.astype(F32)
    gates = jax.nn.sigmoid(gz_ref[...])
    for r in range(GROUP):
        slope = sl_ref[g * GROUP + r]
        s = _dot_nt(qb_sc[:, r * HEAD_DIM:(r + 1) * HEAD_DIM], kw) * SCALE - slope * dist_wf
        o_w = _dot(_softmax_rows(s, mask_w).astype(BF16), vw)
        o_s = acc_sc[r] / l_sc[r]
        o = (gates[:, r:r + 1] * oc_sc[r] + gates[:, GROUP + r:GROUP + r + 1] * o_s
             + gates[:, 2 * GROUP + r:2 * GROUP + r + 1] * o_w)
        o_ref[:, r * HEAD_DIM:(r + 1) * HEAD_DIM] = o.astype(o_ref.dtype)


def _nsa_prompt_attention(z, gz, kc, vc, slopes, batch, seq):
    tq = min(256, seq)
    nq = seq // tq
    n_c = (seq - CMP_LEN) // CMP_STRIDE + 1
    n_s = -(-seq // SEL_BLOCK)
    k_sel = min(N_SEL, n_s)
    rows_c = kc.shape[1]
    sel_lanes = 128 * (-(-n_s // 128))
    wslab = min(WINDOW + tq, seq)
    ov = _overlap_matrix(n_c, n_s, rows_c, sel_lanes)
    gw = Q_DIM // N_KV
    kcol = Q_DIM // HEAD_DIM

    def kv_spec(branch, c):
        base = kcol + (branch * 2 + c) * N_KV
        return pl.BlockSpec((seq, HEAD_DIM), lambda b, g, i, sl: (b, base + g))

    grid_spec = pltpu.PrefetchScalarGridSpec(
        num_scalar_prefetch=1,
        grid=(batch, N_KV, nq),
        in_specs=[pl.BlockSpec((tq, gw), lambda b, g, i, sl: (b * nq + i, g)),
                  kv_spec(1, 0), kv_spec(1, 1), kv_spec(2, 0), kv_spec(2, 1),
                  pl.BlockSpec((None, rows_c, HEAD_DIM), lambda b, g, i, sl: (b * N_KV + g, 0, 0)),
                  pl.BlockSpec((None, rows_c, HEAD_DIM), lambda b, g, i, sl: (b * N_KV + g, 0, 0)),
                  pl.BlockSpec((tq, GATE_LANES), lambda b, g, i, sl: (b * nq + i, g)),
                  pl.BlockSpec((rows_c, sel_lanes), lambda b, g, i, sl: (0, 0))],
        out_specs=pl.BlockSpec((tq, gw), lambda b, g, i, sl: (b * nq + i, g)),
        scratch_shapes=[pltpu.VMEM((tq, gw), BF16),
                        pltpu.VMEM((GROUP, tq, HEAD_DIM), F32),
                        pltpu.VMEM((GROUP, tq, 1), F32),
                        pltpu.VMEM((GROUP, tq, 1), F32),
                        pltpu.VMEM((GROUP, tq, HEAD_DIM), F32)])
    return pl.pallas_call(
        functools.partial(_nsa_prompt_kernel, seq=seq, tq=tq, n_c=n_c, n_s=n_s, k_sel=k_sel, wslab=wslab),
        grid_spec=grid_spec,
        out_shape=jax.ShapeDtypeStruct((batch * seq, Q_DIM), BF16),
        compiler_params=pltpu.CompilerParams(
            dimension_semantics=("parallel", "parallel", "arbitrary"),
            vmem_limit_bytes=V7X_VMEM_LIMIT),
        name="nsa_prompt",
    )(slopes, z, z, z, z, z, kc, vc, gz, ov)


def _nsa_sample_kernel(sl_ref, q_ref, kc_ref, vc_ref, ks_ref, vs_ref, ksn_ref, vsn_ref, kw_ref, vw_ref,
                       kwn_ref, vwn_ref, gz_ref, ov_ref, ex_ref, o_ref, *, past, tnew, n_c, n_s, k_sel):
    g = pl.program_id(1)
    rows = tnew * GROUP
    row = lax.broadcasted_iota(jnp.int32, (rows, 1), 0)
    t_new = row // GROUP
    tpos = past + t_new
    slope = jnp.zeros((rows, 1), F32)
    for r in range(GROUP):
        slope = jnp.where(row % GROUP == r, sl_ref[g * GROUP + r], slope)
    q = q_ref[...].astype(BF16)
    new_lanes = ksn_ref.shape[0]
    i_new = lax.broadcasted_iota(jnp.int32, (1, new_lanes), 1)
    dist_new = t_new - i_new
    mask_new = (i_new < tnew) & (dist_new >= 0)
    dist_new_f = dist_new.astype(F32)

    lanes_c = kc_ref.shape[0]
    n_idx = lax.broadcasted_iota(jnp.int32, (1, lanes_c), 1)
    dist_c = tpos - (n_idx * CMP_STRIDE + (CMP_LEN - 1))
    mask_c = (n_idx < n_c) & (dist_c >= 0)
    s = _dot_nt(q, kc_ref[...].astype(BF16)) * SCALE - slope * dist_c.astype(F32)
    p_c = _softmax_rows(s, mask_c)
    o_c = _dot(p_c.astype(BF16), vc_ref[...].astype(BF16))
    imp = _dot_split3(p_c, ov_ref[...])
    sel_lanes = imp.shape[1]
    imp = jnp.sum(imp.reshape(tnew, GROUP, sel_lanes), axis=1, keepdims=True)
    imp = jnp.broadcast_to(imp, (tnew, GROUP, sel_lanes)).reshape(rows, sel_lanes)
    sel = _select_blocks(imp, tpos // SEL_BLOCK, n_s, k_sel)

    def two_part_attention(s_past, mask_past, v_past, s_new, v_new):
        s_past = jnp.where(mask_past, s_past, NEG)
        s_new = jnp.where(mask_new, s_new, NEG)
        m = jnp.maximum(jnp.max(s_past, axis=-1, keepdims=True), jnp.max(s_new, axis=-1, keepdims=True))
        e_past = jnp.where(mask_past, jnp.exp(s_past - m), 0.0)
        e_new = jnp.where(mask_new, jnp.exp(s_new - m), 0.0)
        denom = jnp.sum(e_past, axis=-1, keepdims=True) + jnp.sum(e_new, axis=-1, keepdims=True)
        return (_dot(e_past.astype(BF16), v_past) + _dot(e_new.astype(BF16), v_new)) / denom

    kpos = lax.broadcasted_iota(jnp.int32, (1, past), 1)
    dist_s = (tpos - kpos).astype(F32)
    mask_s = _dot(sel.astype(BF16), ex_ref[...]) > 0.5
    s_past = _dot_nt(q, ks_ref[...].astype(BF16)) * SCALE - slope * dist_s
    s_new = _dot_nt(q, ksn_ref[...].astype(BF16)) * SCALE - slope * dist_new_f
    new_blk = past // SEL_BLOCK
    sel_new = jnp.sum(jnp.where(lax.broadcasted_iota(jnp.int32, (1, sel_lanes), 1) == new_blk, sel, 0.0),
                      axis=-1, keepdims=True) > 0.5
    mask_sn = sel_new & mask_new
    s_new = jnp.where(mask_sn, s_new, NEG)
    s_past = jnp.where(mask_s, s_past, NEG)
    m = jnp.maximum(jnp.max(s_past, axis=-1, keepdims=True), jnp.max(s_new, axis=-1, keepdims=True))
    e_past = jnp.where(mask_s, jnp.exp(s_past - m), 0.0)
    e_new = jnp.where(mask_sn, jnp.exp(s_new - m), 0.0)
    denom = jnp.sum(e_past, axis=-1, keepdims=True) + jnp.sum(e_new, axis=-1, keepdims=True)
    o_s = (_dot(e_past.astype(BF16), vs_ref[...].astype(BF16))
           + _dot(e_new.astype(BF16), vsn_ref[...].astype(BF16))) / denom

    win_buf = kw_ref.shape[0]
    wpos = (past - win_buf) + lax.broadcasted_iota(jnp.int32, (1, win_buf), 1)
    dist_w = tpos - wpos
    mask_w = (dist_w >= 0) & (dist_w < WINDOW)
    s_wp = _dot_nt(q, kw_ref[...].astype(BF16)) * SCALE - slope * dist_w.astype(F32)
    s_wn = _dot_nt(q, kwn_ref[...].astype(BF16)) * SCALE - slope * dist_new_f
    o_w = two_part_attention(s_wp, mask_w, vw_ref[...].astype(BF16), s_wn, vwn_ref[...].astype(BF16))

    gates = jax.nn.sigmoid(gz_ref[...])
    o_ref[...] = gates[:, 0:1] * o_c + gates[:, 1:2] * o_s + gates[:, 2:3] * o_w


def _nsa_sample_attention(q, kc, vc, slc, slc_new, win, win_new, gz, slopes, tnew):
    batch = q.shape[0]
    rows = q.shape[2]
    past = slc.shape[2]
    total = past + tnew
    n_c = (total - CMP_LEN) // CMP_STRIDE + 1
    n_s = -(-total // SEL_BLOCK)
    k_sel = min(N_SEL, n_s)
    rows_c = kc.shape[1]
    assert n_c <= rows_c and (n_c - 1) * CMP_STRIDE + CMP_LEN <= past
    sel_lanes = 128 * (-(-n_s // 128))
    new_lanes = slc_new.shape[2]
    win_buf = win.shape[1]
    ov = _overlap_matrix(n_c, n_s, rows_c, sel_lanes)
    ex = jnp.asarray(np.arange(sel_lanes)[:, None] == (np.arange(past)[None, :] // SEL_BLOCK), dtype=BF16)

    def bg4(shape):
        return pl.BlockSpec((None, None) + shape, lambda b, g, sl: (b, g, 0, 0))

    def bgv4(shape):
        return pl.BlockSpec((None, None) + shape, lambda b, g, sl: (b, N_KV + g, 0, 0))

    cmp_spec = pl.BlockSpec((None, rows_c, HEAD_DIM), lambda b, g, sl: (b * N_KV + g, 0, 0))
    grid_spec = pltpu.PrefetchScalarGridSpec(
        num_scalar_prefetch=1,
        grid=(batch, N_KV),
        in_specs=[bg4((rows, HEAD_DIM)), cmp_spec, cmp_spec,
                  bg4((past, HEAD_DIM)), bgv4((past, HEAD_DIM)),
                  bg4((new_lanes, HEAD_DIM)), bgv4((new_lanes, HEAD_DIM)),
                  pl.BlockSpec((None, win_buf, HEAD_DIM), lambda b, g, sl: (b, 0, g)),
                  pl.BlockSpec((None, win_buf, HEAD_DIM), lambda b, g, sl: (b, 0, N_KV + g)),
                  bg4((new_lanes, HEAD_DIM)), bgv4((new_lanes, HEAD_DIM)),
                  bg4((rows, GATE_LANES)),
                  pl.BlockSpec((rows_c, sel_lanes), lambda b, g, sl: (0, 0)),
                  pl.BlockSpec((sel_lanes, past), lambda b, g, sl: (0, 0))],
        out_specs=bg4((rows, HEAD_DIM)))
    return pl.pallas_call(
        functools.partial(_nsa_sample_kernel, past=past, tnew=tnew, n_c=n_c, n_s=n_s, k_sel=k_sel),
        grid_spec=grid_spec,
        out_shape=jax.ShapeDtypeStruct((batch, N_KV, rows, HEAD_DIM), F32),
        compiler_params=pltpu.CompilerParams(
            dimension_semantics=("parallel", "arbitrary"),
            vmem_limit_bytes=V7X_VMEM_LIMIT),
        name="nsa_sample",
    )(slopes, q, kc, vc, slc, slc, slc_new, slc_new, win, win, win_new, win_new, gz, ov, ex)


def _gather_kernel(pt_ref, page_ref, o_ref):
    for cg in range(o_ref.shape[0]):
        o_ref[cg] = page_ref[:, cg * HEAD_DIM:(cg + 1) * HEAD_DIM]


def _gather_pages(pool, page_table):
    batch, n_pages = page_table.shape
    planes = pool.shape[2] // HEAD_DIM
    grid_spec = pltpu.PrefetchScalarGridSpec(
        num_scalar_prefetch=1,
        grid=(batch, n_pages),
        in_specs=[pl.BlockSpec((None, PAGE_SIZE, pool.shape[2]), lambda b, p, pt: (pt[b, p], 0, 0))],
        out_specs=pl.BlockSpec((None, planes, PAGE_SIZE, HEAD_DIM), lambda b, p, pt: (b, 0, p, 0)))
    return pl.pallas_call(
        _gather_kernel,
        grid_spec=grid_spec,
        out_shape=jax.ShapeDtypeStruct((batch, planes, n_pages * PAGE_SIZE, HEAD_DIM), pool.dtype),
        name="gather_pages",
    )(page_table, pool)


def _conv_kernel(zb_ref, zc_ref, zu_ref, prev_ref, ck_ref, y_ref, last_ref, *, t_real):
    v = zc_ref[...] * zu_ref[...]
    t = v.shape[0]
    row = lax.broadcasted_iota(jnp.int32, (t, 1), 0)
    v_m1 = jnp.where(row == 0, prev_ref[1:2, :], pltpu.roll(v, 1, 0))
    v_m2 = jnp.where(row == 0, prev_ref[0:1, :], jnp.where(row == 1, prev_ref[1:2, :], pltpu.roll(v, 2, 0)))
    y = ck_ref[0:1, :] * v_m2 + ck_ref[1:2, :] * v_m1 + ck_ref[2:3, :] * v
    y_ref[...] = (zb_ref[...] * y).astype(y_ref.dtype)
    last_ref[...] = v[t_real - (CONV_W - 1):t_real]


def _conv_mix(z, prev, conv_k, out_dtype, t_real):
    assert CONV_W == 3 and t_real >= CONV_W - 1
    n_seq, t, d3 = z.shape
    d = d3 // 3
    tc = 512
    nc = d // tc
    return pl.pallas_call(
        functools.partial(_conv_kernel, t_real=t_real),
        grid=(n_seq, nc),
        in_specs=[pl.BlockSpec((None, t, tc), lambda s, c: (s, 0, c)),
                  pl.BlockSpec((None, t, tc), lambda s, c: (s, 0, nc + c)),
                  pl.BlockSpec((None, t, tc), lambda s, c: (s, 0, 2 * nc + c)),
                  pl.BlockSpec((None, CONV_W - 1, tc), lambda s, c: (s, 0, c)),
                  pl.BlockSpec((CONV_W, tc), lambda s, c: (0, c))],
        out_specs=[pl.BlockSpec((None, t, tc), lambda s, c: (s, 0, c)),
                   pl.BlockSpec((None, CONV_W - 1, tc), lambda s, c: (s, 0, c))],
        out_shape=[jax.ShapeDtypeStruct((n_seq, t, d), out_dtype),
                   jax.ShapeDtypeStruct((n_seq, CONV_W - 1, d), F32)],
        compiler_params=pltpu.CompilerParams(vmem_limit_bytes=V7X_VMEM_LIMIT),
        name="conv_mix",
    )(z, z, z, prev, conv_k)


def _nsa_weights(w_in, pe, w1, w2, w_out):
    n_main = Q_DIM + 3 * KV_DIM
    w_main = w_in[:, :n_main].astype(BF16)
    wg = w_in[:, n_main:].reshape(-1, 3, N_KV, GROUP).transpose(0, 2, 1, 3).reshape(-1, N_KV, 3 * GROUP)
    wg = jnp.pad(wg, ((0, 0), (0, 0), (0, GATE_LANES - 3 * GROUP))).reshape(-1, N_KV * GATE_LANES)
    return (w_main, wg.astype(BF16)) + _compress_weights(pe, w1, w2) + (w_out.astype(BF16),)


def _split_kv(z, lead):
    kv = z[:, Q_DIM:].reshape(lead + (3, 2, N_KV, HEAD_DIM))
    return kv[..., 0, :, :, :], kv[..., 1, :, :, :], kv[..., 2, :, :, :]


def _nsa_prompt_layer(x, h, weights, slopes, batch, seq, win_buf):
    w_main, w_gate, pe2, w1f, w2b, w_out = weights
    z = _gemm(h, w_main)
    gz = _gemm(h, w_gate)
    kv_c, kv_s, kv_w = _split_kv(z, (batch, seq))
    n_chunks = seq // CMP_STRIDE
    xc = kv_c.transpose(2, 0, 3, 1, 4).reshape(1, 2, batch * N_KV * n_chunks, CMP_STRIDE * HEAD_DIM)
    comp = _compress(xc, pe2, w1f, w2b, rows=min(4, batch * N_KV) * n_chunks)
    comp = comp.reshape(2, batch * N_KV, n_chunks, HEAD_DIM)
    o = _nsa_prompt_attention(z, gz, comp[0], comp[1], slopes, batch, seq)
    x = _gemm(o, w_out, res=x)
    pad = max(win_buf - seq, 0)
    kv_w_pad = jnp.pad(kv_w, ((0, 0), (pad, 0), (0, 0), (0, 0), (0, 0)))
    return x, kv_c, kv_s, kv_w_pad[:, kv_w_pad.shape[1] - win_buf:]


def _nsa_sample_layer(x, h, weights, slopes, pool_c, pool_s, win_state, page_table, batch, tnew):
    w_main, w_gate, pe2, w1f, w2b, w_out = weights
    z = _gemm(h, w_main, tm=batch * tnew)
    gz = _gemm(h, w_gate, tm=batch * tnew)
    kv_c, kv_s, kv_w = _split_kv(z, (batch, tnew))
    n_pool = pool_c.shape[0]
    cmp_dense = _gather_pages(pool_c.reshape(n_pool, PAGE_SIZE, KV_DIM), page_table)
    slc_dense = _gather_pages(pool_s.reshape(n_pool, PAGE_SIZE, KV_DIM), page_table)
    past = cmp_dense.shape[2]
    n_chunks = past // CMP_STRIDE
    assert past % SEL_BLOCK == 0 and tnew <= SEL_BLOCK
    xc = cmp_dense.reshape(batch, 2, N_KV * n_chunks, CMP_STRIDE * HEAD_DIM)
    comp = _compress(xc, pe2, w1f, w2b, rows=n_chunks).reshape(2, batch * N_KV, n_chunks, HEAD_DIM)

    def new_rows(kv):
        kv = kv.transpose(0, 2, 3, 1, 4).reshape(batch, 2 * N_KV, tnew, HEAD_DIM)
        return jnp.pad(kv, ((0, 0), (0, 0), (0, 128 - tnew), (0, 0)))

    q = z[:, :Q_DIM].reshape(batch, tnew, N_KV, GROUP, HEAD_DIM).transpose(0, 2, 1, 3, 4)
    q = q.reshape(batch, N_KV, tnew * GROUP, HEAD_DIM)
    gq = gz.reshape(batch, tnew, N_KV, GATE_LANES)[..., :3 * GROUP].reshape(batch, tnew, N_KV, 3, GROUP)
    gq = gq.transpose(0, 2, 1, 4, 3).reshape(batch, N_KV, tnew * GROUP, 3)
    gq = jnp.pad(gq, ((0, 0), (0, 0), (0, 0), (0, GATE_LANES - 3)))
    win_buf = win_state.shape[1]
    o = _nsa_sample_attention(q, comp[0], comp[1], slc_dense, new_rows(kv_s),
                              win_state.reshape(batch, win_buf, KV_DIM), new_rows(kv_w), gq, slopes, tnew)
    o = o.reshape(batch, N_KV, tnew, GROUP, HEAD_DIM).transpose(0, 2, 1, 3, 4).reshape(batch * tnew, Q_DIM)
    x = _gemm(o.astype(BF16), w_out, res=x, tm=batch * tnew)
    kw_full = jnp.concatenate([win_state, kv_w], axis=1)
    return x, kv_c, kv_s, kw_full[:, kw_full.shape[1] - win_buf:]


def _conv_layer(x, h, prev, w_in, conv_k, w_out, n_seq, t, y_dtype):
    d = x.shape[1]
    z = _gemm(h, w_in, tm=min(1024, n_seq * t)).reshape(n_seq, t, 3 * d)
    t_pad = -(-t // 8) * 8
    z = jnp.pad(z, ((0, 0), (0, t_pad - t), (0, 0)))
    y, last = _conv_mix(z, prev, conv_k, y_dtype, t)
    y = y[:, :t].reshape(n_seq * t, d).astype(BF16)
    x = _gemm(y, w_out, res=x, tm=min(1024, n_seq * t))
    return x, last


def _mlp(x, g, w_up, w_down):
    m = x.shape[0]
    h = _rmsnorm(x, g, BF16)
    a = _gemm(h, w_up, act="relu2", out_dtype=BF16, tm=min(1024, m))
    return _gemm(a, w_down, res=x, tm=min(1024, m), tk=2048)


def kernel(x_prompt, x_sample, cache_kv_cmp, cache_kv_slc, state_win_kv, state_conv, page_table, norm_mix, norm_mlp, norm_final, nsa_w_in, nsa_cmp_pe, nsa_cmp_w1, nsa_cmp_w2, nsa_w_out, conv_w_in, conv_kernel, conv_w_out, mlp_w_up, mlp_w_down):
    batch, seq, d = x_prompt.shape
    dec_batch, dec_seq, _ = x_sample.shape
    depth = norm_mix.shape[0]
    win_buf = state_win_kv.shape[2]
    slopes = _alibi_slopes()
    xp = x_prompt.reshape(batch * seq, d)
    xs = x_sample.reshape(dec_batch * dec_seq, d)
    kvc_p, kvc_s, kvs_p, kvs_s, win_p, win_s, cv_p, cv_s = [], [], [], [], [], [], [], []
    for i in range(depth):
        j = i // 2
        hp = _rmsnorm(xp, norm_mix[i], BF16)
        hs = _rmsnorm(xs, norm_mix[i], BF16)
        if i % 2 == 0:
            weights = _nsa_weights(nsa_w_in[j], nsa_cmp_pe[j], nsa_cmp_w1[j], nsa_cmp_w2[j], nsa_w_out[j])
            xp, a_p, b_p, w_p = _nsa_prompt_layer(xp, hp, weights, slopes, batch, seq, win_buf)
            xs, a_s, b_s, w_s = _nsa_sample_layer(xs, hs, weights, slopes, cache_kv_cmp[j], cache_kv_slc[j],
                                                  state_win_kv[j], page_table, dec_batch, dec_seq)
            kvc_p.append(a_p); kvc_s.append(a_s)
            kvs_p.append(b_p); kvs_s.append(b_s)
            win_p.append(w_p); win_s.append(w_s)
        else:
            w_in = conv_w_in[j].astype(BF16)
            w_out = conv_w_out[j].astype(BF16)
            zeros = jnp.zeros((batch, CONV_W - 1, d), F32)
            xp, c_p = _conv_layer(xp, hp, zeros, w_in, conv_kernel[j], w_out, batch, seq, BF16)
            xs, c_s = _conv_layer(xs, hs, state_conv[j], w_in, conv_kernel[j], w_out, dec_batch, dec_seq, F32)
            cv_p.append(c_p); cv_s.append(c_s)
        w_up = mlp_w_up[i].astype(BF16)
        w_down = mlp_w_down[i].astype(BF16)
        xp = _mlp(xp, norm_mlp[i], w_up, w_down)
        xs = _mlp(xs, norm_mlp[i], w_up, w_down)
    y_prompt = _rmsnorm(xp, norm_final, F32).reshape(batch, seq, d)
    y_sample = _rmsnorm(xs, norm_final, F32).reshape(dec_batch, dec_seq, d)
    return (y_prompt, y_sample, jnp.stack(kvc_p), jnp.stack(kvc_s), jnp.stack(kvs_p), jnp.stack(kvs_s),
            jnp.stack(win_p), jnp.stack(win_s), jnp.stack(cv_p), jnp.stack(cv_s))
```

```python
import functools

import numpy as np
import jax
import jax.numpy as jnp
from jax import lax
from jax.experimental import pallas as pl
from jax.experimental.pallas import tpu as pltpu

N_HEADS = 32
HEAD_DIM = 128
N_KV = 4
GROUP = N_HEADS // N_KV
CMP_LEN = 32
CMP_STRIDE = 16
SEL_BLOCK = 64
N_SEL = 16
WINDOW = 512
CONV_W = 3
PAGE_SIZE = 128
RMS_EPS = 1e-6
NEG = -1e30
BIG = 1e30
SCALE = HEAD_DIM ** -0.5
LOG2E = 1.4426950408889634
Q_DIM = N_HEADS * HEAD_DIM
KV_DIM = 2 * N_KV * HEAD_DIM
GATE_LANES = 128

V7X_VMEM_LIMIT = 56 * 1024 * 1024
BF16 = jnp.bfloat16
F32 = jnp.float32
NT_DIMS = (((1,), (1,)), ((), ()))


def _alibi_slopes():
    i = np.arange(1, N_HEADS + 1, dtype=np.float32)
    return jnp.asarray(2.0 ** (-8.0 * i / N_HEADS), dtype=jnp.float32)


def _dot(a, b):
    return jnp.dot(a, b, preferred_element_type=F32)


def _dot_nt(a, b):
    return lax.dot_general(a, b, NT_DIMS, preferred_element_type=F32)


def _dot_split3(x, m):
    hi = x.astype(BF16)
    r1 = x - hi.astype(F32)
    mid = r1.astype(BF16)
    lo = (r1 - mid.astype(F32)).astype(BF16)
    return _dot(hi, m) + _dot(mid, m) + _dot(lo, m)


def _dot_split3_nt(m, x):
    hi = x.astype(BF16)
    r1 = x - hi.astype(F32)
    mid = r1.astype(BF16)
    lo = (r1 - mid.astype(F32)).astype(BF16)
    return _dot_nt(m, hi) + _dot_nt(m, mid) + _dot_nt(m, lo)


def _rms_kernel(x_ref, g_ref, o_ref):
    x = x_ref[...]
    ms = jnp.mean(x * x, axis=-1, keepdims=True)
    o_ref[...] = (x * lax.rsqrt(ms + RMS_EPS) * g_ref[...]).astype(o_ref.dtype)


def _rmsnorm(x, g, out_dtype):
    m, d = x.shape
    tm = min(m, 256)
    return pl.pallas_call(
        _rms_kernel,
        grid=(m // tm,),
        in_specs=[pl.BlockSpec((tm, d), lambda i: (i, 0)),
                  pl.BlockSpec((1, d), lambda i: (0, 0))],
        out_specs=pl.BlockSpec((tm, d), lambda i: (i, 0)),
        out_shape=jax.ShapeDtypeStruct((m, d), out_dtype),
        name="rmsnorm",
    )(x, g.reshape(1, d))


def _gemm_kernel(*refs, nk, act, has_res):
    a_ref, w_ref = refs[0], refs[1]
    res_ref = refs[2] if has_res else None
    o_ref = refs[3] if has_res else refs[2]

    def finish(acc):
        if act == "relu2":
            acc = jnp.maximum(acc, 0.0)
            acc = acc * acc
        if has_res:
            acc = acc + res_ref[...]
        o_ref[...] = acc.astype(o_ref.dtype)

    if nk == 1:
        finish(_dot(a_ref[...], w_ref[...]))
        return

    acc_ref = refs[-1]
    k = pl.program_id(2)
    part = _dot(a_ref[...], w_ref[...])

    @pl.when(k == 0)
    def _():
        acc_ref[...] = part

    @pl.when(k > 0)
    def _():
        acc_ref[...] += part

    @pl.when(k == nk - 1)
    def _():
        finish(acc_ref[...])


def _gemm(a, w, layer, *, res=None, act=None, out_dtype=F32, tm=1024, tn=1024, tk=None):
    m, kdim = a.shape
    n = w.shape[2]
    tm, tn = min(tm, m), min(tn, n)
    tk = kdim if tk is None else min(tk, kdim)
    assert m % tm == 0 and n % tn == 0 and kdim % tk == 0
    nk = kdim // tk
    in_specs = [pl.BlockSpec((tm, tk), lambda i, j, k: (i, k)),
                pl.BlockSpec((None, tk, tn), lambda i, j, k: (layer, k, j))]
    args = [a, w]
    if res is not None:
        in_specs.append(pl.BlockSpec((tm, tn), lambda i, j, k: (i, j)))
        args.append(res)
    return pl.pallas_call(
        functools.partial(_gemm_kernel, nk=nk, act=act, has_res=res is not None),
        grid=(m // tm, n // tn, nk),
        in_specs=in_specs,
        out_specs=pl.BlockSpec((tm, tn), lambda i, j, k: (i, j)),
        out_shape=jax.ShapeDtypeStruct((m, n), out_dtype),
        scratch_shapes=[pltpu.VMEM((tm, tn), F32)] if nk > 1 else [],
        compiler_params=pltpu.CompilerParams(
            dimension_semantics=("parallel", "parallel", "arbitrary"),
            vmem_limit_bytes=V7X_VMEM_LIMIT),
        name="gemm",
    )(*args)


def _compress_kernel(*refs):
    x_refs, (pe_ref, w1_ref, w2_ref, o_ref) = refs[:-4], refs[-4:]
    planes = len(x_refs)
    n_chunks = x_refs[0].shape[0] // CMP_STRIDE
    half = CMP_STRIDE * HEAD_DIM
    x = jnp.concatenate(
        [jnp.concatenate([x_ref[pl.ds(l, n_chunks, stride=CMP_STRIDE), :] for l in range(CMP_STRIDE)], axis=1)
         for x_ref in x_refs], axis=0)
    lo = _dot((x + pe_ref[0:1, :]).astype(BF16), w1_ref[0:half, :])
    hi = _dot((x + pe_ref[1:2, :]).astype(BF16), w1_ref[half:2 * half, :])
    hid = jax.nn.gelu(lo + pltpu.roll(hi, planes * n_chunks - 1, 0))
    out = _dot(hid.astype(BF16), w2_ref[...])
    for p in range(planes):
        o_ref[p] = out[p * n_chunks:(p + 1) * n_chunks]


def _compress_weights(pe, w1, w2):
    n_layers = pe.shape[0]
    half = CMP_STRIDE * HEAD_DIM
    pe2 = pe.reshape(n_layers, 2, CMP_LEN // CMP_STRIDE, half)
    w1f = w1.reshape(n_layers, 2, CMP_LEN * HEAD_DIM, w1.shape[-1]).astype(BF16)
    return pe2, w1f, w2.astype(BF16)


def _compress(x, x_specs, grid, out_index, t, n_seq, cmp_weights, layer):
    pe2, w1f, w2b = cmp_weights
    half = CMP_STRIDE * HEAD_DIM
    hidden = w1f.shape[-1]
    n_chunks = t // CMP_STRIDE
    planes = len(x_specs)

    def weight_spec(shape):
        return pl.BlockSpec((None, None) + shape, lambda c, *_: (layer, c, 0, 0))

    return pl.pallas_call(
        _compress_kernel,
        grid=(2,) + grid,
        in_specs=x_specs + [weight_spec((2, half)), weight_spec((2 * half, hidden)), weight_spec((hidden, HEAD_DIM))],
        out_specs=pl.BlockSpec((None, planes, n_chunks, HEAD_DIM), lambda c, *i: (c, out_index(*i), 0, 0)),
        out_shape=jax.ShapeDtypeStruct((2, n_seq, n_chunks, HEAD_DIM), F32),
        compiler_params=pltpu.CompilerParams(vmem_limit_bytes=V7X_VMEM_LIMIT),
        name="compress",
    )(*([x] * planes), pe2, w1f, w2b)


def _compress_prompt(z, batch, seq, cmp_weights, layer):
    kcol = Q_DIM // HEAD_DIM

    def plane_spec(g):
        return pl.BlockSpec((seq, HEAD_DIM), lambda c, b: (b, kcol + c * N_KV + g))

    return _compress(z, [plane_spec(g) for g in range(N_KV)], (batch,), lambda b: b, seq, batch * N_KV,
                     cmp_weights, layer)


def _compress_planes(x, cmp_weights, layer):
    batch, _, t, _ = x.shape
    x_spec = pl.BlockSpec((None, None, t, HEAD_DIM), lambda c, b, g: (b, c * N_KV + g, 0, 0))
    return _compress(x, [x_spec], (batch, N_KV), lambda b, g: b * N_KV + g, t, batch * N_KV, cmp_weights, layer)


def _softmax_rows(s, mask):
    s = jnp.where(mask, s, NEG)
    m = jnp.max(s, axis=-1, keepdims=True)
    e = jnp.exp(s - m)
    return jnp.where(mask, e / jnp.sum(e, axis=-1, keepdims=True), 0.0)


def _select_blocks(imp, blk_t, n_s, k_sel):
    lanes = imp.shape[-1]
    s_idx = lax.broadcasted_iota(jnp.int32, (1, lanes), 1)
    forced = (s_idx == 0) | (s_idx == blk_t) | (s_idx == blk_t - 1)
    imp = jnp.where(s_idx <= blk_t, jnp.where(forced, BIG, imp), NEG)
    imp = jnp.where(s_idx < n_s, imp, -jnp.inf)
    s_f = s_idx.astype(F32)
    sel = jnp.zeros(imp.shape, F32)
    for _ in range(k_sel):
        top = jnp.max(imp, axis=-1, keepdims=True)
        first = jnp.min(jnp.where(imp == top, s_f, float(lanes)), axis=-1, keepdims=True)
        hit = s_f == first
        sel = jnp.where(hit, 1.0, sel)
        imp = jnp.where(hit, -jnp.inf, imp)
    return sel


def _select_blocks_t(imp_t, blk_t, n_s, k_sel):
    rows = imp_t.shape[0]
    s_idx = lax.broadcasted_iota(jnp.int32, (rows, 1), 0)
    forced = (s_idx == 0) | (s_idx == blk_t) | (s_idx == blk_t - 1)
    imp = jnp.where(s_idx <= blk_t, jnp.where(forced, BIG, imp_t), NEG)
    imp = jnp.where(s_idx < n_s, imp, -jnp.inf)
    rank = jnp.zeros(imp.shape, F32)
    for other in range(n_s):
        row = imp[other:other + 1, :]
        wins_tie = jnp.where(s_idx > other, 1.0, 0.0)
        rank = rank + jnp.where(row > imp, 1.0, jnp.where(row == imp, wins_tie, 0.0))
    return jnp.where(rank < k_sel, 1.0, 0.0)


def _overlap_matrix(n_c, n_s, rows, cols):
    ci = np.arange(rows)[:, None]
    sj = np.arange(cols)[None, :]
    ov = ((ci * CMP_STRIDE <= sj * SEL_BLOCK + SEL_BLOCK - 1)
          & (ci * CMP_STRIDE + CMP_LEN - 1 >= sj * SEL_BLOCK)
          & (ci < n_c) & (sj < n_s))
    return jnp.asarray(ov, dtype=BF16)


def _nsa_prompt_kernel(sl_ref, q_ref, ks_ref, vs_ref, kw_ref, vw_ref, kc_ref, vc_ref, gz_ref, ovt_ref,
                       o_ref, qs_sc, p_sc, oc_sc, m_sc, acc_sc, *, seq, tq, n_c, n_s, k_sel, wslab):
    g = pl.program_id(1)
    qi = pl.program_id(2)
    t0 = qi * tq
    tpos = t0 + lax.broadcasted_iota(jnp.int32, (tq, 1), 0)
    q_scaled = q_ref[...] * (SCALE * LOG2E)
    for r in range(GROUP):
        qs_sc[r * tq:(r + 1) * tq, 0:HEAD_DIM] = q_scaled[:, r * HEAD_DIM:(r + 1) * HEAD_DIM].astype(BF16)
    slopes2 = [sl_ref[g * GROUP + r] * LOG2E for r in range(GROUP)]
    ones = jnp.ones((tq, HEAD_DIM), BF16)

    def head(x, r):
        return x[r * tq:(r + 1) * tq]

    lanes_c = kc_ref.shape[0]
    n_idx = lax.broadcasted_iota(jnp.int32, (1, lanes_c), 1)
    cpos = n_idx * CMP_STRIDE + (CMP_LEN - 1)
    mask_c = (n_idx < n_c) & (tpos >= cpos)
    cpos_f = cpos.astype(F32)
    s_all = _dot_nt(qs_sc[:, 0:HEAD_DIM], kc_ref[...].astype(BF16))
    psum = jnp.zeros((tq, lanes_c), F32)
    for r in range(GROUP):
        s = jnp.where(mask_c, head(s_all, r) + slopes2[r] * cpos_f, NEG)
        e = jnp.exp2(s - jnp.max(s, axis=-1, keepdims=True))
        p = jnp.where(mask_c, e / jnp.sum(e, axis=-1, keepdims=True), 0.0)
        p_sc[r * tq:(r + 1) * tq, 0:lanes_c] = p.astype(BF16)
        psum = psum + p
    oc_sc[...] = _dot(p_sc[:, 0:lanes_c], vc_ref[...].astype(BF16))
    sel_lanes = ovt_ref.shape[0]
    s_rows = 8 * (-(-n_s // 8))
    imp_t = _dot_split3_nt(ovt_ref[...], psum)[0:s_rows]
    blk_t = (t0 + lax.broadcasted_iota(jnp.int32, (1, tq), 1)) // SEL_BLOCK
    sel_t = _select_blocks_t(imp_t, blk_t, n_s, k_sel)
    sel = jnp.concatenate([sel_t, jnp.zeros((sel_lanes - s_rows, tq), F32)], axis=0).T
    sel_bias = jnp.where(sel > 0.5, 0.0, NEG).astype(BF16)
    for r in range(GROUP):
        qs_sc[r * tq:(r + 1) * tq, HEAD_DIM:HEAD_DIM + sel_lanes] = sel_bias

    m_sc[...] = jnp.full(m_sc.shape, NEG, F32)
    acc_sc[...] = jnp.zeros(acc_sc.shape, F32)

    def key_tile(j, diagonal):
        k0 = pl.multiple_of(j * tq, tq)
        kpos = k0 + lax.broadcasted_iota(jnp.int32, (1, tq), 1)
        key_blk = (k0 + lax.broadcasted_iota(jnp.int32, (tq, sel_lanes), 0)) // SEL_BLOCK
        one_hot = jnp.where(lax.broadcasted_iota(jnp.int32, (tq, sel_lanes), 1) == key_blk, 1.0, 0.0)
        ka = jnp.concatenate([ks_ref[pl.ds(k0, tq), :].astype(BF16), one_hot.astype(BF16)], axis=1)
        va = jnp.concatenate([vs_ref[pl.ds(k0, tq), :].astype(BF16), ones], axis=1)
        kpos_f = kpos.astype(F32)
        causal = jnp.where(tpos >= kpos, 0.0, NEG) if diagonal else None
        s_all = _dot_nt(qs_sc[...], ka)
        alphas = []
        for r in range(GROUP):
            s = head(s_all, r) + slopes2[r] * kpos_f
            if diagonal:
                s = s + causal
            m_old = head(m_sc, r)
            m_new = jnp.maximum(m_old, jnp.broadcast_to(jnp.max(s, axis=-1, keepdims=True), m_old.shape))
            alphas.append(jnp.exp2(m_old - m_new))
            p_sc[r * tq:(r + 1) * tq, 0:tq] = jnp.exp2(s - jnp.concatenate([m_new, m_new], axis=1)).astype(BF16)
            m_sc[r * tq:(r + 1) * tq, :] = m_new
        pv = _dot(p_sc[:, 0:tq], va)
        for r in range(GROUP):
            alpha2 = jnp.concatenate([alphas[r], alphas[r]], axis=1)
            acc_sc[r * tq:(r + 1) * tq, :] = alpha2 * head(acc_sc, r) + head(pv, r)

    def full_tile(j, carry):
        key_tile(j, False)
        return carry

    lax.fori_loop(0, qi, full_tile, 0)
    key_tile(qi, True)

    w0 = jnp.clip(t0 - WINDOW, 0, seq - wslab)
    w0 = pl.multiple_of(w0, tq)
    kw = kw_ref[pl.ds(w0, wslab), :].astype(BF16)
    vwa = jnp.concatenate([vw_ref[pl.ds(w0, wslab), :].astype(BF16), jnp.ones((wslab, HEAD_DIM), BF16)], axis=1)
    wpos = w0 + lax.broadcasted_iota(jnp.int32, (1, wslab), 1)
    dist_w = tpos - wpos
    band = jnp.where((dist_w >= 0) & (dist_w < WINDOW), 0.0, NEG)
    wpos_f = wpos.astype(F32)
    s_all = _dot_nt(qs_sc[:, 0:HEAD_DIM], kw)
    for r in range(GROUP):
        s = head(s_all, r) + (band + slopes2[r] * wpos_f)
        p_sc[r * tq:(r + 1) * tq, :] = jnp.exp2(s - jnp.max(s, axis=-1, keepdims=True)).astype(BF16)
    ow_all = _dot(p_sc[...], vwa)
    gates = jax.nn.sigmoid(gz_ref[...])
    for r in range(GROUP):
        ow = head(ow_all, r)
        acc = head(acc_sc, r)
        o = (gates[:, r:r + 1] * head(oc_sc, r)
             + gates[:, GROUP + r:GROUP + r + 1] * (acc[:, :HEAD_DIM] / acc[:, HEAD_DIM:])
             + gates[:, 2 * GROUP + r:2 * GROUP + r + 1] * (ow[:, :HEAD_DIM] / ow[:, HEAD_DIM:]))
        o_ref[:, r * HEAD_DIM:(r + 1) * HEAD_DIM] = o.astype(o_ref.dtype)


def _nsa_prompt_attention(z, gz, comp, slopes, batch, seq):
    tq = min(256, seq)
    nq = seq // tq
    n_c = (seq - CMP_LEN) // CMP_STRIDE + 1
    n_s = -(-seq // SEL_BLOCK)
    k_sel = min(N_SEL, n_s)
    rows_c = comp.shape[2]
    sel_lanes = 128 * (-(-n_s // 128))
    wslab = min(WINDOW + tq, seq)
    assert wslab >= tq and wslab >= rows_c
    ov = _overlap_matrix(n_c, n_s, rows_c, sel_lanes)
    gw = Q_DIM // N_KV
    kcol = Q_DIM // HEAD_DIM

    def kv_spec(branch, c):
        base = kcol + (branch * 2 + c) * N_KV
        return pl.BlockSpec((seq, HEAD_DIM), lambda b, g, i, sl: (b, base + g))

    grid_spec = pltpu.PrefetchScalarGridSpec(
        num_scalar_prefetch=1,
        grid=(batch, N_KV, nq),
        in_specs=[pl.BlockSpec((tq, gw), lambda b, g, i, sl: (b * nq + i, g)),
                  kv_spec(1, 0), kv_spec(1, 1), kv_spec(2, 0), kv_spec(2, 1),
                  pl.BlockSpec((None, None, rows_c, HEAD_DIM), lambda b, g, i, sl: (0, b * N_KV + g, 0, 0)),
                  pl.BlockSpec((None, None, rows_c, HEAD_DIM), lambda b, g, i, sl: (1, b * N_KV + g, 0, 0)),
                  pl.BlockSpec((tq, GATE_LANES), lambda b, g, i, sl: (b * nq + i, g)),
                  pl.BlockSpec((sel_lanes, rows_c), lambda b, g, i, sl: (0, 0))],
        out_specs=pl.BlockSpec((tq, gw), lambda b, g, i, sl: (b * nq + i, g)),
        scratch_shapes=[pltpu.VMEM((GROUP * tq, HEAD_DIM + sel_lanes), BF16),
                        pltpu.VMEM((GROUP * tq, wslab), BF16),
                        pltpu.VMEM((GROUP * tq, HEAD_DIM), F32),
                        pltpu.VMEM((GROUP * tq, HEAD_DIM), F32),
                        pltpu.VMEM((GROUP * tq, 2 * HEAD_DIM), F32)])
    return pl.pallas_call(
        functools.partial(_nsa_prompt_kernel, seq=seq, tq=tq, n_c=n_c, n_s=n_s, k_sel=k_sel, wslab=wslab),
        grid_spec=grid_spec,
        out_shape=jax.ShapeDtypeStruct((batch * seq, Q_DIM), BF16),
        compiler_params=pltpu.CompilerParams(
            dimension_semantics=("parallel", "parallel", "arbitrary"),
            vmem_limit_bytes=V7X_VMEM_LIMIT),
        name="nsa_prompt",
    )(slopes, z, z, z, z, z, comp, comp, gz, ov.T)


def _nsa_sample_kernel(sl_ref, q_ref, kc_ref, vc_ref, ks_ref, vs_ref, ksn_ref, vsn_ref, kw_ref, vw_ref,
                       kwn_ref, vwn_ref, gz_ref, ov_ref, ex_ref, o_ref, *, past, tnew, n_c, n_s, k_sel):
    g = pl.program_id(1)
    rows = tnew * GROUP
    row = lax.broadcasted_iota(jnp.int32, (rows, 1), 0)
    t_new = row // GROUP
    tpos = past + t_new
    slope = jnp.zeros((rows, 1), F32)
    for r in range(GROUP):
        slope = jnp.where(row % GROUP == r, sl_ref[g * GROUP + r], slope)
    q = q_ref[...].astype(BF16)
    new_lanes = ksn_ref.shape[0]
    i_new = lax.broadcasted_iota(jnp.int32, (1, new_lanes), 1)
    dist_new = t_new - i_new
    mask_new = (i_new < tnew) & (dist_new >= 0)
    dist_new_f = dist_new.astype(F32)

    lanes_c = kc_ref.shape[0]
    n_idx = lax.broadcasted_iota(jnp.int32, (1, lanes_c), 1)
    dist_c = tpos - (n_idx * CMP_STRIDE + (CMP_LEN - 1))
    mask_c = (n_idx < n_c) & (dist_c >= 0)
    s = _dot_nt(q, kc_ref[...].astype(BF16)) * SCALE - slope * dist_c.astype(F32)
    p_c = _softmax_rows(s, mask_c)
    o_c = _dot(p_c.astype(BF16), vc_ref[...].astype(BF16))
    imp = _dot_split3(p_c, ov_ref[...])
    sel_lanes = imp.shape[1]
    imp = jnp.sum(imp.reshape(tnew, GROUP, sel_lanes), axis=1, keepdims=True)
    imp = jnp.broadcast_to(imp, (tnew, GROUP, sel_lanes)).reshape(rows, sel_lanes)
    sel = _select_blocks(imp, tpos // SEL_BLOCK, n_s, k_sel)

    def two_part_attention(s_past, mask_past, v_past, s_new, v_new):
        s_past = jnp.where(mask_past, s_past, NEG)
        s_new = jnp.where(mask_new, s_new, NEG)
        m = jnp.maximum(jnp.max(s_past, axis=-1, keepdims=True), jnp.max(s_new, axis=-1, keepdims=True))
        e_past = jnp.where(mask_past, jnp.exp(s_past - m), 0.0)
        e_new = jnp.where(mask_new, jnp.exp(s_new - m), 0.0)
        denom = jnp.sum(e_past, axis=-1, keepdims=True) + jnp.sum(e_new, axis=-1, keepdims=True)
        return (_dot(e_past.astype(BF16), v_past) + _dot(e_new.astype(BF16), v_new)) / denom

    kpos = lax.broadcasted_iota(jnp.int32, (1, past), 1)
    dist_s = (tpos - kpos).astype(F32)
    mask_s = _dot(sel.astype(BF16), ex_ref[...]) > 0.5
    s_past = _dot_nt(q, ks_ref[...].astype(BF16)) * SCALE - slope * dist_s
    s_new = _dot_nt(q, ksn_ref[...].astype(BF16)) * SCALE - slope * dist_new_f
    new_blk = past // SEL_BLOCK
    sel_new = jnp.sum(jnp.where(lax.broadcasted_iota(jnp.int32, (1, sel_lanes), 1) == new_blk, sel, 0.0),
                      axis=-1, keepdims=True) > 0.5
    mask_sn = sel_new & mask_new
    s_new = jnp.where(mask_sn, s_new, NEG)
    s_past = jnp.where(mask_s, s_past, NEG)
    m = jnp.maximum(jnp.max(s_past, axis=-1, keepdims=True), jnp.max(s_new, axis=-1, keepdims=True))
    e_past = jnp.where(mask_s, jnp.exp(s_past - m), 0.0)
    e_new = jnp.where(mask_sn, jnp.exp(s_new - m), 0.0)
    denom = jnp.sum(e_past, axis=-1, keepdims=True) + jnp.sum(e_new, axis=-1, keepdims=True)
    o_s = (_dot(e_past.astype(BF16), vs_ref[...].astype(BF16))
           + _dot(e_new.astype(BF16), vsn_ref[...].astype(BF16))) / denom

    win_buf = kw_ref.shape[0]
    wpos = (past - win_buf) + lax.broadcasted_iota(jnp.int32, (1, win_buf), 1)
    dist_w = tpos - wpos
    mask_w = (dist_w >= 0) & (dist_w < WINDOW)
    s_wp = _dot_nt(q, kw_ref[...].astype(BF16)) * SCALE - slope * dist_w.astype(F32)
    s_wn = _dot_nt(q, kwn_ref[...].astype(BF16)) * SCALE - slope * dist_new_f
    o_w = two_part_attention(s_wp, mask_w, vw_ref[...].astype(BF16), s_wn, vwn_ref[...].astype(BF16))

    gates = jax.nn.sigmoid(gz_ref[...])
    o_ref[...] = gates[:, 0:1] * o_c + gates[:, 1:2] * o_s + gates[:, 2:3] * o_w


def _nsa_sample_attention(q, comp, slc, slc_new, win, win_new, gz, slopes, tnew):
    batch = q.shape[0]
    rows = q.shape[2]
    past = slc.shape[2]
    total = past + tnew
    n_c = (total - CMP_LEN) // CMP_STRIDE + 1
    n_s = -(-total // SEL_BLOCK)
    k_sel = min(N_SEL, n_s)
    rows_c = comp.shape[2]
    assert n_c <= rows_c and (n_c - 1) * CMP_STRIDE + CMP_LEN <= past
    sel_lanes = 128 * (-(-n_s // 128))
    new_lanes = slc_new.shape[2]
    win_buf = win.shape[1]
    ov = _overlap_matrix(n_c, n_s, rows_c, sel_lanes)
    ex = jnp.asarray(np.arange(sel_lanes)[:, None] == (np.arange(past)[None, :] // SEL_BLOCK), dtype=BF16)

    def bg4(shape):
        return pl.BlockSpec((None, None) + shape, lambda b, g, sl: (b, g, 0, 0))

    def bgv4(shape):
        return pl.BlockSpec((None, None) + shape, lambda b, g, sl: (b, N_KV + g, 0, 0))

    def cmp_spec(c):
        return pl.BlockSpec((None, None, rows_c, HEAD_DIM), lambda b, g, sl: (c, b * N_KV + g, 0, 0))

    grid_spec = pltpu.PrefetchScalarGridSpec(
        num_scalar_prefetch=1,
        grid=(batch, N_KV),
        in_specs=[bg4((rows, HEAD_DIM)), cmp_spec(0), cmp_spec(1),
                  bg4((past, HEAD_DIM)), bgv4((past, HEAD_DIM)),
                  bg4((new_lanes, HEAD_DIM)), bgv4((new_lanes, HEAD_DIM)),
                  pl.BlockSpec((None, win_buf, HEAD_DIM), lambda b, g, sl: (b, 0, g)),
                  pl.BlockSpec((None, win_buf, HEAD_DIM), lambda b, g, sl: (b, 0, N_KV + g)),
                  bg4((new_lanes, HEAD_DIM)), bgv4((new_lanes, HEAD_DIM)),
                  bg4((rows, GATE_LANES)),
                  pl.BlockSpec((rows_c, sel_lanes), lambda b, g, sl: (0, 0)),
                  pl.BlockSpec((sel_lanes, past), lambda b, g, sl: (0, 0))],
        out_specs=bg4((rows, HEAD_DIM)))
    return pl.pallas_call(
        functools.partial(_nsa_sample_kernel, past=past, tnew=tnew, n_c=n_c, n_s=n_s, k_sel=k_sel),
        grid_spec=grid_spec,
        out_shape=jax.ShapeDtypeStruct((batch, N_KV, rows, HEAD_DIM), F32),
        compiler_params=pltpu.CompilerParams(
            dimension_semantics=("parallel", "arbitrary"),
            vmem_limit_bytes=V7X_VMEM_LIMIT),
        name="nsa_sample",
    )(slopes, q, comp, comp, slc, slc, slc_new, slc_new, win, win, win_new, win_new, gz, ov, ex)


GATHER_PAGES_PER_STEP = 4


def _gather_kernel(pt_ref, *refs):
    page_refs, o_ref = refs[:-1], refs[-1]
    for k, page_ref in enumerate(page_refs):
        for c in range(2):
            for g in range(N_KV):
                o_ref[c * N_KV + g, k * PAGE_SIZE:(k + 1) * PAGE_SIZE, :] = page_ref[:, c, g, :]


def _gather_pages(pool, page_table, layer):
    batch, n_pages = page_table.shape
    pps = GATHER_PAGES_PER_STEP
    assert n_pages % pps == 0
    planes = 2 * N_KV

    def page_spec(k):
        return pl.BlockSpec((None, None, PAGE_SIZE, 2, N_KV, HEAD_DIM),
                            lambda b, p, pt: (layer, pt[b, p * pps + k], 0, 0, 0, 0))

    grid_spec = pltpu.PrefetchScalarGridSpec(
        num_scalar_prefetch=1,
        grid=(batch, n_pages // pps),
        in_specs=[page_spec(k) for k in range(pps)],
        out_specs=pl.BlockSpec((None, planes, pps * PAGE_SIZE, HEAD_DIM), lambda b, p, pt: (b, 0, p, 0)))
    return pl.pallas_call(
        _gather_kernel,
        grid_spec=grid_spec,
        out_shape=jax.ShapeDtypeStruct((batch, planes, n_pages * PAGE_SIZE, HEAD_DIM), pool.dtype),
        name="gather_pages",
    )(page_table, *([pool] * pps))


def _conv_kernel(zb_ref, zc_ref, zu_ref, prev_ref, ck_ref, y_ref, last_ref, *, t_real):
    v = zc_ref[...] * zu_ref[...]
    t = v.shape[0]
    row = lax.broadcasted_iota(jnp.int32, (t, 1), 0)
    v_m1 = jnp.where(row == 0, prev_ref[1:2, :], pltpu.roll(v, 1, 0))
    v_m2 = jnp.where(row == 0, prev_ref[0:1, :], jnp.where(row == 1, prev_ref[1:2, :], pltpu.roll(v, 2, 0)))
    y = ck_ref[0:1, :] * v_m2 + ck_ref[1:2, :] * v_m1 + ck_ref[2:3, :] * v
    y_ref[...] = (zb_ref[...] * y).astype(y_ref.dtype)
    last_ref[...] = v[t_real - (CONV_W - 1):t_real]


def _conv_mix(z, prev, conv_k, out_dtype, t_real):
    assert CONV_W == 3 and t_real >= CONV_W - 1
    n_seq, t, d3 = z.shape
    d = d3 // 3
    tc = 512
    nc = d // tc
    return pl.pallas_call(
        functools.partial(_conv_kernel, t_real=t_real),
        grid=(n_seq, nc),
        in_specs=[pl.BlockSpec((None, t, tc), lambda s, c: (s, 0, c)),
                  pl.BlockSpec((None, t, tc), lambda s, c: (s, 0, nc + c)),
                  pl.BlockSpec((None, t, tc), lambda s, c: (s, 0, 2 * nc + c)),
                  pl.BlockSpec((None, CONV_W - 1, tc), lambda s, c: (s, 0, c)),
                  pl.BlockSpec((CONV_W, tc), lambda s, c: (0, c))],
        out_specs=[pl.BlockSpec((None, t, tc), lambda s, c: (s, 0, c)),
                   pl.BlockSpec((None, CONV_W - 1, tc), lambda s, c: (s, 0, c))],
        out_shape=[jax.ShapeDtypeStruct((n_seq, t, d), out_dtype),
                   jax.ShapeDtypeStruct((n_seq, CONV_W - 1, d), F32)],
        compiler_params=pltpu.CompilerParams(vmem_limit_bytes=V7X_VMEM_LIMIT),
        name="conv_mix",
    )(z, z, z, prev, conv_k)


def _nsa_weights(w_in, pe, w1, w2, w_out):
    n_layers, d, _ = w_in.shape
    n_main = Q_DIM + 3 * KV_DIM
    w_main = w_in[:, :, :n_main].astype(BF16)
    wg = w_in[:, :, n_main:].reshape(n_layers, d, 3, N_KV, GROUP).transpose(0, 1, 3, 2, 4)
    wg = wg.reshape(n_layers, d, N_KV, 3 * GROUP)
    wg = jnp.pad(wg, ((0, 0), (0, 0), (0, 0), (0, GATE_LANES - 3 * GROUP))).reshape(n_layers, d, N_KV * GATE_LANES)
    return w_main, wg.astype(BF16), _compress_weights(pe, w1, w2), w_out.astype(BF16)


def _split_kv(z, lead):
    kv = z[:, Q_DIM:].reshape(lead + (3, 2, N_KV, HEAD_DIM))
    return kv[..., 0, :, :, :], kv[..., 1, :, :, :], kv[..., 2, :, :, :]


def _nsa_prompt_layer(x, h, weights, layer, slopes, batch, seq, win_buf):
    w_main, w_gate, cmp_weights, w_out = weights
    z = _gemm(h, w_main, layer)
    gz = _gemm(h, w_gate, layer)
    kv_c, kv_s, kv_w = _split_kv(z, (batch, seq))
    comp = _compress_prompt(z, batch, seq, cmp_weights, layer)
    o = _nsa_prompt_attention(z, gz, comp, slopes, batch, seq)
    x = _gemm(o, w_out, layer, res=x)
    pad = max(win_buf - seq, 0)
    kv_w_pad = jnp.pad(kv_w, ((0, 0), (pad, 0), (0, 0), (0, 0), (0, 0)))
    return x, kv_c, kv_s, kv_w_pad[:, kv_w_pad.shape[1] - win_buf:]


def _nsa_sample_layer(x, h, weights, layer, slopes, pools_c, pools_s, win_state, page_table, batch, tnew):
    w_main, w_gate, cmp_weights, w_out = weights
    z = _gemm(h, w_main, layer, tm=batch * tnew)
    gz = _gemm(h, w_gate, layer, tm=batch * tnew)
    kv_c, kv_s, kv_w = _split_kv(z, (batch, tnew))
    cmp_dense = _gather_pages(pools_c, page_table, layer)
    slc_dense = _gather_pages(pools_s, page_table, layer)
    past = cmp_dense.shape[2]
    assert past % SEL_BLOCK == 0 and tnew <= SEL_BLOCK
    comp = _compress_planes(cmp_dense, cmp_weights, layer)

    def new_rows(kv):
        kv = kv.transpose(0, 2, 3, 1, 4).reshape(batch, 2 * N_KV, tnew, HEAD_DIM)
        return jnp.pad(kv, ((0, 0), (0, 0), (0, 128 - tnew), (0, 0)))

    q = z[:, :Q_DIM].reshape(batch, tnew, N_KV, GROUP, HEAD_DIM).transpose(0, 2, 1, 3, 4)
    q = q.reshape(batch, N_KV, tnew * GROUP, HEAD_DIM)
    gq = gz.reshape(batch, tnew, N_KV, GATE_LANES)[..., :3 * GROUP].reshape(batch, tnew, N_KV, 3, GROUP)
    gq = gq.transpose(0, 2, 1, 4, 3).reshape(batch, N_KV, tnew * GROUP, 3)
    gq = jnp.pad(gq, ((0, 0), (0, 0), (0, 0), (0, GATE_LANES - 3)))
    win_buf = win_state.shape[1]
    o = _nsa_sample_attention(q, comp, slc_dense, new_rows(kv_s),
                              win_state.reshape(batch, win_buf, KV_DIM), new_rows(kv_w), gq, slopes, tnew)
    o = o.reshape(batch, N_KV, tnew, GROUP, HEAD_DIM).transpose(0, 2, 1, 3, 4).reshape(batch * tnew, Q_DIM)
    x = _gemm(o.astype(BF16), w_out, layer, res=x, tm=batch * tnew)
    kw_full = jnp.concatenate([win_state, kv_w], axis=1)
    return x, kv_c, kv_s, kw_full[:, kw_full.shape[1] - win_buf:]


def _conv_layer(x, h, prev, w_in, conv_k, w_out, layer, n_seq, t, y_dtype):
    d = x.shape[1]
    z = _gemm(h, w_in, layer, tm=min(1024, n_seq * t)).reshape(n_seq, t, 3 * d)
    t_pad = -(-t // 8) * 8
    z = jnp.pad(z, ((0, 0), (0, t_pad - t), (0, 0)))
    y, last = _conv_mix(z, prev, conv_k, y_dtype, t)
    y = y[:, :t].reshape(n_seq * t, d).astype(BF16)
    x = _gemm(y, w_out, layer, res=x, tm=min(1024, n_seq * t))
    return x, last


def _mlp(x, g, w_up, w_down, layer):
    m = x.shape[0]
    h = _rmsnorm(x, g, BF16)
    a = _gemm(h, w_up, layer, act="relu2", out_dtype=BF16, tm=min(1024, m))
    return _gemm(a, w_down, layer, res=x, tm=min(1024, m), tk=2048)


def kernel(x_prompt, x_sample, cache_kv_cmp, cache_kv_slc, state_win_kv, state_conv, page_table, norm_mix, norm_mlp, norm_final, nsa_w_in, nsa_cmp_pe, nsa_cmp_w1, nsa_cmp_w2, nsa_w_out, conv_w_in, conv_kernel, conv_w_out, mlp_w_up, mlp_w_down):
    batch, seq, d = x_prompt.shape
    dec_batch, dec_seq, _ = x_sample.shape
    depth = norm_mix.shape[0]
    win_buf = state_win_kv.shape[2]
    slopes = _alibi_slopes()
    xp = x_prompt.reshape(batch * seq, d)
    xs = x_sample.reshape(dec_batch * dec_seq, d)
    kvc_p, kvc_s, kvs_p, kvs_s, win_p, win_s, cv_p, cv_s = [], [], [], [], [], [], [], []
    nsa_weights = _nsa_weights(nsa_w_in, nsa_cmp_pe, nsa_cmp_w1, nsa_cmp_w2, nsa_w_out)
    cw_in, cw_out = conv_w_in.astype(BF16), conv_w_out.astype(BF16)
    w_up, w_down = mlp_w_up.astype(BF16), mlp_w_down.astype(BF16)
    for i in range(depth):
        j = i // 2
        hp = _rmsnorm(xp, norm_mix[i], BF16)
        hs = _rmsnorm(xs, norm_mix[i], BF16)
        if i % 2 == 0:
            xp, a_p, b_p, w_p = _nsa_prompt_layer(xp, hp, nsa_weights, j, slopes, batch, seq, win_buf)
            xs, a_s, b_s, w_s = _nsa_sample_layer(xs, hs, nsa_weights, j, slopes, cache_kv_cmp, cache_kv_slc,
                                                  state_win_kv[j], page_table, dec_batch, dec_seq)
            kvc_p.append(a_p); kvc_s.append(a_s)
            kvs_p.append(b_p); kvs_s.append(b_s)
            win_p.append(w_p); win_s.append(w_s)
        else:
            zeros = jnp.zeros((batch, CONV_W - 1, d), F32)
            xp, c_p = _conv_layer(xp, hp, zeros, cw_in, conv_kernel[j], cw_out, j, batch, seq, BF16)
            xs, c_s = _conv_layer(xs, hs, state_conv[j], cw_in, conv_kernel[j], cw_out, j, dec_batch, dec_seq, F32)
            cv_p.append(c_p); cv_s.append(c_s)
        xp = _mlp(xp, norm_mlp[i], w_up, w_down, i)
        xs = _mlp(xs, norm_mlp[i], w_up, w_down, i)
    y_prompt = _rmsnorm(xp, norm_final, F32).reshape(batch, seq, d)
    y_sample = _rmsnorm(xs, norm_final, F32).reshape(dec_batch, dec_seq, d)
    return (y_prompt, y_sample, jnp.stack(kvc_p), jnp.stack(kvc_s), jnp.stack(kvs_p), jnp.stack(kvs_s),
            jnp.stack(win_p), jnp.stack(win_s), jnp.stack(cv_p), jnp.stack(cv_s))
```

```python
import functools

import numpy as np
import jax
import jax.numpy as jnp
from jax import lax
from jax.experimental import pallas as pl
from jax.experimental.pallas import tpu as pltpu

N_HEADS = 32
HEAD_DIM = 128
N_KV = 4
GROUP = N_HEADS // N_KV
CMP_LEN = 32
CMP_STRIDE = 16
SEL_BLOCK = 64
N_SEL = 16
WINDOW = 512
CONV_W = 3
PAGE_SIZE = 128
RMS_EPS = 1e-6
NEG = -1e30
BIG = 1e30
SCALE = HEAD_DIM ** -0.5
LOG2E = 1.4426950408889634
Q_DIM = N_HEADS * HEAD_DIM
KV_DIM = 2 * N_KV * HEAD_DIM
GATE_LANES = 128

V7X_VMEM_LIMIT = 60 * 1024 * 1024
BF16 = jnp.bfloat16
F32 = jnp.float32
NT_DIMS = (((1,), (1,)), ((), ()))


def _alibi_slopes():
    i = np.arange(1, N_HEADS + 1, dtype=np.float32)
    return jnp.asarray(2.0 ** (-8.0 * i / N_HEADS), dtype=jnp.float32)


def _dot(a, b):
    return jnp.dot(a, b, preferred_element_type=F32)


def _dot_nt(a, b):
    return lax.dot_general(a, b, NT_DIMS, preferred_element_type=F32)


def _dot_split3(x, m):
    hi = x.astype(BF16)
    r1 = x - hi.astype(F32)
    mid = r1.astype(BF16)
    lo = (r1 - mid.astype(F32)).astype(BF16)
    return _dot(hi, m) + _dot(mid, m) + _dot(lo, m)


def _dot_split3_nt(m, x):
    hi = x.astype(BF16)
    r1 = x - hi.astype(F32)
    mid = r1.astype(BF16)
    lo = (r1 - mid.astype(F32)).astype(BF16)
    return _dot_nt(m, hi) + _dot_nt(m, mid) + _dot_nt(m, lo)


def _rms_kernel(x_ref, g_ref, o_ref):
    x = x_ref[...]
    ms = jnp.mean(x * x, axis=-1, keepdims=True)
    o_ref[...] = (x * lax.rsqrt(ms + RMS_EPS) * g_ref[...]).astype(o_ref.dtype)


def _rmsnorm(x, g, out_dtype):
    m, d = x.shape
    tm = min(m, 256)
    return pl.pallas_call(
        _rms_kernel,
        grid=(m // tm,),
        in_specs=[pl.BlockSpec((tm, d), lambda i: (i, 0)),
                  pl.BlockSpec((1, d), lambda i: (0, 0))],
        out_specs=pl.BlockSpec((tm, d), lambda i: (i, 0)),
        out_shape=jax.ShapeDtypeStruct((m, d), out_dtype),
        name="rmsnorm",
    )(x, g.reshape(1, d))


def _gemm_kernel(*refs, nk, act, has_res, emit_bf16):
    a_ref, w_ref = refs[0], refs[1]
    res_ref = refs[2] if has_res else None
    o_ref = refs[3] if has_res else refs[2]
    w = w_ref[...]
    if emit_bf16:
        w = w.astype(BF16)
        refs[-1][...] = w

    def finish(acc):
        if act == "relu2":
            acc = jnp.maximum(acc, 0.0)
            acc = acc * acc
        if has_res:
            acc = acc + res_ref[...]
        o_ref[...] = acc.astype(o_ref.dtype)

    if nk == 1:
        finish(_dot(a_ref[...], w))
        return

    k = pl.program_id(2)
    part = _dot(a_ref[...], w)

    @pl.when(k == 0)
    def _():
        o_ref[...] = part + res_ref[...] if has_res else part

    @pl.when(k > 0)
    def _():
        o_ref[...] += part


def _gemm(a, w, layer=None, *, n=None, res=None, act=None, out_dtype=F32, tm=1024, tn=1024, tk=None,
          emit_bf16=False):
    m, kdim = a.shape
    n = w.shape[-1] if n is None else n
    tm, tn = min(tm, m), min(tn, n)
    tk = kdim if tk is None else min(tk, kdim)
    assert m % tm == 0 and n % tn == 0 and kdim % tk == 0
    nk = kdim // tk
    assert nk == 1 or (act is None and out_dtype == F32)
    assert not emit_bf16 or m == tm
    if layer is None:
        w_spec = pl.BlockSpec((tk, tn), lambda i, j, k: (k, j))
    else:
        w_spec = pl.BlockSpec((None, tk, tn), lambda i, j, k: (layer, k, j))
    in_specs = [pl.BlockSpec((tm, tk), lambda i, j, k: (i, k)), w_spec]
    args = [a, w]
    if res is not None:
        in_specs.append(pl.BlockSpec((tm, tn), lambda i, j, k: (i, j)))
        args.append(res)
    out_specs = pl.BlockSpec((tm, tn), lambda i, j, k: (i, j))
    out_shape = jax.ShapeDtypeStruct((m, n), out_dtype)
    if emit_bf16:
        out_specs = [out_specs, pl.BlockSpec((tk, tn), lambda i, j, k: (k, j))]
        out_shape = [out_shape, jax.ShapeDtypeStruct((kdim, n), BF16)]
    return pl.pallas_call(
        functools.partial(_gemm_kernel, nk=nk, act=act, has_res=res is not None, emit_bf16=emit_bf16),
        grid=(m // tm, n // tn, nk),
        in_specs=in_specs,
        out_specs=out_specs,
        out_shape=out_shape,
        compiler_params=pltpu.CompilerParams(
            dimension_semantics=("parallel", "parallel", "arbitrary"),
            vmem_limit_bytes=V7X_VMEM_LIMIT),
        name="gemm",
    )(*args)


def _compress_kernel(*refs):
    x_refs, (pe_ref, w1_ref, w2_ref, o_ref) = refs[:-4], refs[-4:]
    planes = len(x_refs)
    n_chunks = x_refs[0].shape[0] // CMP_STRIDE
    half = CMP_STRIDE * HEAD_DIM
    x = jnp.concatenate(
        [jnp.concatenate([x_ref[pl.ds(l, n_chunks, stride=CMP_STRIDE), :] for l in range(CMP_STRIDE)], axis=1)
         for x_ref in x_refs], axis=0)
    lo = _dot((x + pe_ref[0:1, :]).astype(BF16), w1_ref[0:half, :])
    hi = _dot((x + pe_ref[1:2, :]).astype(BF16), w1_ref[half:2 * half, :])
    hid = jax.nn.gelu(lo + pltpu.roll(hi, planes * n_chunks - 1, 0))
    out = _dot(hid.astype(BF16), w2_ref[...])
    for p in range(planes):
        o_ref[p] = out[p * n_chunks:(p + 1) * n_chunks]


def _compress_weights(pe, w1, w2):
    n_layers = pe.shape[0]
    half = CMP_STRIDE * HEAD_DIM
    pe2 = pe.reshape(n_layers, 2, CMP_LEN // CMP_STRIDE, half)
    w1f = w1.reshape(n_layers, 2, CMP_LEN * HEAD_DIM, w1.shape[-1]).astype(BF16)
    return pe2, w1f, w2.astype(BF16)


def _compress(x, x_specs, grid, out_index, t, n_seq, cmp_weights, layer):
    pe2, w1f, w2b = cmp_weights
    half = CMP_STRIDE * HEAD_DIM
    hidden = w1f.shape[-1]
    n_chunks = t // CMP_STRIDE
    planes = len(x_specs)

    def weight_spec(shape):
        return pl.BlockSpec((None, None) + shape, lambda c, *_: (layer, c, 0, 0))

    return pl.pallas_call(
        _compress_kernel,
        grid=(2,) + grid,
        in_specs=x_specs + [weight_spec((2, half)), weight_spec((2 * half, hidden)), weight_spec((hidden, HEAD_DIM))],
        out_specs=pl.BlockSpec((None, planes, n_chunks, HEAD_DIM), lambda c, *i: (c, out_index(*i), 0, 0)),
        out_shape=jax.ShapeDtypeStruct((2, n_seq, n_chunks, HEAD_DIM), F32),
        compiler_params=pltpu.CompilerParams(vmem_limit_bytes=V7X_VMEM_LIMIT),
        name="compress",
    )(*([x] * planes), pe2, w1f, w2b)


def _compress_prompt(z, batch, seq, cmp_weights, layer):
    kcol = Q_DIM // HEAD_DIM

    def plane_spec(g):
        return pl.BlockSpec((seq, HEAD_DIM), lambda c, b: (b, kcol + c * N_KV + g))

    return _compress(z, [plane_spec(g) for g in range(N_KV)], (batch,), lambda b: b, seq, batch * N_KV,
                     cmp_weights, layer)


def _compress_planes(x, cmp_weights, layer):
    batch, _, t, _ = x.shape
    x_spec = pl.BlockSpec((None, None, t, HEAD_DIM), lambda c, b, g: (b, c * N_KV + g, 0, 0))
    return _compress(x, [x_spec], (batch, N_KV), lambda b, g: b * N_KV + g, t, batch * N_KV, cmp_weights, layer)


def _softmax_rows(s, mask):
    s = jnp.where(mask, s, NEG)
    m = jnp.max(s, axis=-1, keepdims=True)
    e = jnp.exp(s - m)
    return jnp.where(mask, e / jnp.sum(e, axis=-1, keepdims=True), 0.0)


def _select_blocks(imp, blk_t, n_s, k_sel):
    lanes = imp.shape[-1]
    s_idx = lax.broadcasted_iota(jnp.int32, (1, lanes), 1)
    forced = (s_idx == 0) | (s_idx == blk_t) | (s_idx == blk_t - 1)
    imp = jnp.where(s_idx <= blk_t, jnp.where(forced, BIG, imp), NEG)
    imp = jnp.where(s_idx < n_s, imp, -jnp.inf)
    s_f = s_idx.astype(F32)
    sel = jnp.zeros(imp.shape, F32)
    for _ in range(k_sel):
        top = jnp.max(imp, axis=-1, keepdims=True)
        first = jnp.min(jnp.where(imp == top, s_f, float(lanes)), axis=-1, keepdims=True)
        hit = s_f == first
        sel = jnp.where(hit, 1.0, sel)
        imp = jnp.where(hit, -jnp.inf, imp)
    return sel


def _select_blocks_t(imp_t, blk_t, n_s, k_sel):
    rows = imp_t.shape[0]
    s_idx = lax.broadcasted_iota(jnp.int32, (rows, 1), 0)
    forced = (s_idx == 0) | (s_idx == blk_t) | (s_idx == blk_t - 1)
    imp = jnp.where(s_idx <= blk_t, jnp.where(forced, BIG, imp_t), NEG)
    imp = jnp.where(s_idx < n_s, imp, -jnp.inf)
    rank = jnp.zeros(imp.shape, F32)
    for other in range(n_s):
        row = imp[other:other + 1, :]
        wins_tie = jnp.where(s_idx > other, 1.0, 0.0)
        rank = rank + jnp.where(row > imp, 1.0, jnp.where(row == imp, wins_tie, 0.0))
    return jnp.where(rank < k_sel, 1.0, 0.0)


def _overlap_matrix(n_c, n_s, rows, cols):
    ci = np.arange(rows)[:, None]
    sj = np.arange(cols)[None, :]
    ov = ((ci * CMP_STRIDE <= sj * SEL_BLOCK + SEL_BLOCK - 1)
          & (ci * CMP_STRIDE + CMP_LEN - 1 >= sj * SEL_BLOCK)
          & (ci < n_c) & (sj < n_s))
    return jnp.asarray(ov, dtype=BF16)


def _nsa_prompt_kernel(sl_ref, q_ref, ks_ref, vs_ref, kw_ref, vw_ref, kc_ref, vc_ref, gz_ref, ovt_ref,
                       o_ref, qs_sc, p_sc, oc_sc, m_sc, acc_sc, *, seq, tq, n_c, n_s, k_sel, wslab):
    g = pl.program_id(1)
    qi = pl.program_id(2)
    t0 = qi * tq
    tpos = t0 + lax.broadcasted_iota(jnp.int32, (tq, 1), 0)
    q_scaled = q_ref[...] * (SCALE * LOG2E)
    for r in range(GROUP):
        qs_sc[r * tq:(r + 1) * tq, 0:HEAD_DIM] = q_scaled[:, r * HEAD_DIM:(r + 1) * HEAD_DIM].astype(BF16)
    slopes2 = [sl_ref[g * GROUP + r] * LOG2E for r in range(GROUP)]
    ones = jnp.ones((tq, HEAD_DIM), BF16)

    def head(x, r):
        return x[r * tq:(r + 1) * tq]

    lanes_c = kc_ref.shape[0]
    n_idx = lax.broadcasted_iota(jnp.int32, (1, lanes_c), 1)
    cpos = n_idx * CMP_STRIDE + (CMP_LEN - 1)
    mask_c = (n_idx < n_c) & (tpos >= cpos)
    cpos_f = cpos.astype(F32)
    s_all = _dot_nt(qs_sc[:, 0:HEAD_DIM], kc_ref[...].astype(BF16))
    psum = jnp.zeros((tq, lanes_c), F32)
    for r in range(GROUP):
        s = jnp.where(mask_c, head(s_all, r) + slopes2[r] * cpos_f, NEG)
        e = jnp.exp2(s - jnp.max(s, axis=-1, keepdims=True))
        p = jnp.where(mask_c, e / jnp.sum(e, axis=-1, keepdims=True), 0.0)
        p_sc[r * tq:(r + 1) * tq, 0:lanes_c] = p.astype(BF16)
        psum = psum + p
    oc_sc[...] = _dot(p_sc[:, 0:lanes_c], vc_ref[...].astype(BF16))
    sel_lanes = ovt_ref.shape[0]
    s_rows = 8 * (-(-n_s // 8))
    imp_t = _dot_split3_nt(ovt_ref[...], psum)[0:s_rows]
    blk_t = (t0 + lax.broadcasted_iota(jnp.int32, (1, tq), 1)) // SEL_BLOCK
    sel_t = _select_blocks_t(imp_t, blk_t, n_s, k_sel)
    sel = jnp.concatenate([sel_t, jnp.zeros((sel_lanes - s_rows, tq), F32)], axis=0).T
    sel_bias = jnp.where(sel > 0.5, 0.0, NEG).astype(BF16)
    for r in range(GROUP):
        qs_sc[r * tq:(r + 1) * tq, HEAD_DIM:HEAD_DIM + sel_lanes] = sel_bias

    m_sc[...] = jnp.full(m_sc.shape, NEG, F32)
    acc_sc[...] = jnp.zeros(acc_sc.shape, F32)

    def key_tile(j, diagonal):
        k0 = pl.multiple_of(j * tq, tq)
        kpos = k0 + lax.broadcasted_iota(jnp.int32, (1, tq), 1)
        key_blk = (k0 + lax.broadcasted_iota(jnp.int32, (tq, sel_lanes), 0)) // SEL_BLOCK
        one_hot = jnp.where(lax.broadcasted_iota(jnp.int32, (tq, sel_lanes), 1) == key_blk, 1.0, 0.0)
        ka = jnp.concatenate([ks_ref[pl.ds(k0, tq), :].astype(BF16), one_hot.astype(BF16)], axis=1)
        va = jnp.concatenate([vs_ref[pl.ds(k0, tq), :].astype(BF16), ones], axis=1)
        kpos_f = kpos.astype(F32)
        causal = jnp.where(tpos >= kpos, 0.0, NEG) if diagonal else None
        s_all = _dot_nt(qs_sc[...], ka)
        alphas = []
        for r in range(GROUP):
            s = head(s_all, r) + slopes2[r] * kpos_f
            if diagonal:
                s = s + causal
            m_old = head(m_sc, r)
            m_new = jnp.maximum(m_old, jnp.broadcast_to(jnp.max(s, axis=-1, keepdims=True), m_old.shape))
            alphas.append(jnp.exp2(m_old - m_new))
            p_sc[r * tq:(r + 1) * tq, 0:tq] = jnp.exp2(s - jnp.concatenate([m_new, m_new], axis=1)).astype(BF16)
            m_sc[r * tq:(r + 1) * tq, :] = m_new
        pv = _dot(p_sc[:, 0:tq], va)
        for r in range(GROUP):
            alpha2 = jnp.concatenate([alphas[r], alphas[r]], axis=1)
            acc_sc[r * tq:(r + 1) * tq, :] = alpha2 * head(acc_sc, r) + head(pv, r)

    def full_tile(j, carry):
        key_tile(j, False)
        return carry

    lax.fori_loop(0, qi, full_tile, 0)
    key_tile(qi, True)

    w0 = jnp.clip(t0 - WINDOW, 0, seq - wslab)
    w0 = pl.multiple_of(w0, tq)
    kw = kw_ref[pl.ds(w0, wslab), :].astype(BF16)
    vwa = jnp.concatenate([vw_ref[pl.ds(w0, wslab), :].astype(BF16), jnp.ones((wslab, HEAD_DIM), BF16)], axis=1)
    wpos = w0 + lax.broadcasted_iota(jnp.int32, (1, wslab), 1)
    dist_w = tpos - wpos
    band = jnp.where((dist_w >= 0) & (dist_w < WINDOW), 0.0, NEG)
    wpos_f = wpos.astype(F32)
    s_all = _dot_nt(qs_sc[:, 0:HEAD_DIM], kw)
    for r in range(GROUP):
        s = head(s_all, r) + (band + slopes2[r] * wpos_f)
        p_sc[r * tq:(r + 1) * tq, :] = jnp.exp2(s - jnp.max(s, axis=-1, keepdims=True)).astype(BF16)
    ow_all = _dot(p_sc[...], vwa)
    gates = jax.nn.sigmoid(gz_ref[...])
    for r in range(GROUP):
        ow = head(ow_all, r)
        acc = head(acc_sc, r)
        o = (gates[:, r:r + 1] * head(oc_sc, r)
             + gates[:, GROUP + r:GROUP + r + 1] * (acc[:, :HEAD_DIM] / acc[:, HEAD_DIM:])
             + gates[:, 2 * GROUP + r:2 * GROUP + r + 1] * (ow[:, :HEAD_DIM] / ow[:, HEAD_DIM:]))
        o_ref[:, r * HEAD_DIM:(r + 1) * HEAD_DIM] = o.astype(o_ref.dtype)


def _nsa_prompt_attention(z, gz, comp, slopes, batch, seq):
    tq = min(256, seq)
    nq = seq // tq
    n_c = (seq - CMP_LEN) // CMP_STRIDE + 1
    n_s = -(-seq // SEL_BLOCK)
    k_sel = min(N_SEL, n_s)
    rows_c = comp.shape[2]
    sel_lanes = 128 * (-(-n_s // 128))
    wslab = min(WINDOW + tq, seq)
    assert wslab >= tq and wslab >= rows_c
    ov = _overlap_matrix(n_c, n_s, rows_c, sel_lanes)
    gw = Q_DIM // N_KV
    kcol = Q_DIM // HEAD_DIM

    def kv_spec(branch, c):
        base = kcol + (branch * 2 + c) * N_KV
        return pl.BlockSpec((seq, HEAD_DIM), lambda b, g, i, sl: (b, base + g))

    grid_spec = pltpu.PrefetchScalarGridSpec(
        num_scalar_prefetch=1,
        grid=(batch, N_KV, nq),
        in_specs=[pl.BlockSpec((tq, gw), lambda b, g, i, sl: (b * nq + i, g)),
                  kv_spec(1, 0), kv_spec(1, 1), kv_spec(2, 0), kv_spec(2, 1),
                  pl.BlockSpec((None, None, rows_c, HEAD_DIM), lambda b, g, i, sl: (0, b * N_KV + g, 0, 0)),
                  pl.BlockSpec((None, None, rows_c, HEAD_DIM), lambda b, g, i, sl: (1, b * N_KV + g, 0, 0)),
                  pl.BlockSpec((tq, GATE_LANES), lambda b, g, i, sl: (b * nq + i, g)),
                  pl.BlockSpec((sel_lanes, rows_c), lambda b, g, i, sl: (0, 0))],
        out_specs=pl.BlockSpec((tq, gw), lambda b, g, i, sl: (b * nq + i, g)),
        scratch_shapes=[pltpu.VMEM((GROUP * tq, HEAD_DIM + sel_lanes), BF16),
                        pltpu.VMEM((GROUP * tq, wslab), BF16),
                        pltpu.VMEM((GROUP * tq, HEAD_DIM), F32),
                        pltpu.VMEM((GROUP * tq, HEAD_DIM), F32),
                        pltpu.VMEM((GROUP * tq, 2 * HEAD_DIM), F32)])
    return pl.pallas_call(
        functools.partial(_nsa_prompt_kernel, seq=seq, tq=tq, n_c=n_c, n_s=n_s, k_sel=k_sel, wslab=wslab),
        grid_spec=grid_spec,
        out_shape=jax.ShapeDtypeStruct((batch * seq, Q_DIM), BF16),
        compiler_params=pltpu.CompilerParams(
            dimension_semantics=("parallel", "parallel", "arbitrary"),
            vmem_limit_bytes=V7X_VMEM_LIMIT),
        name="nsa_prompt",
    )(slopes, z, z, z, z, z, comp, comp, gz, ov.T)


def _nsa_sample_kernel(sl_ref, q_ref, kc_ref, vc_ref, ks_ref, vs_ref, ksn_ref, vsn_ref, kw_ref, vw_ref,
                       kwn_ref, vwn_ref, gz_ref, ov_ref, ex_ref, o_ref, *, past, tnew, n_c, n_s, k_sel):
    g = pl.program_id(1)
    rows = tnew * GROUP
    row = lax.broadcasted_iota(jnp.int32, (rows, 1), 0)
    t_new = row // GROUP
    tpos = past + t_new
    slope = jnp.zeros((rows, 1), F32)
    for r in range(GROUP):
        slope = jnp.where(row % GROUP == r, sl_ref[g * GROUP + r], slope)
    q = q_ref[...].astype(BF16)
    new_lanes = ksn_ref.shape[0]
    i_new = lax.broadcasted_iota(jnp.int32, (1, new_lanes), 1)
    dist_new = t_new - i_new
    mask_new = (i_new < tnew) & (dist_new >= 0)
    dist_new_f = dist_new.astype(F32)

    lanes_c = kc_ref.shape[0]
    n_idx = lax.broadcasted_iota(jnp.int32, (1, lanes_c), 1)
    dist_c = tpos - (n_idx * CMP_STRIDE + (CMP_LEN - 1))
    mask_c = (n_idx < n_c) & (dist_c >= 0)
    s = _dot_nt(q, kc_ref[...].astype(BF16)) * SCALE - slope * dist_c.astype(F32)
    p_c = _softmax_rows(s, mask_c)
    o_c = _dot(p_c.astype(BF16), vc_ref[...].astype(BF16))
    imp = _dot_split3(p_c, ov_ref[...])
    sel_lanes = imp.shape[1]
    imp = jnp.sum(imp.reshape(tnew, GROUP, sel_lanes), axis=1, keepdims=True)
    imp = jnp.broadcast_to(imp, (tnew, GROUP, sel_lanes)).reshape(rows, sel_lanes)
    sel = _select_blocks(imp, tpos // SEL_BLOCK, n_s, k_sel)

    def two_part_attention(s_past, mask_past, v_past, s_new, v_new):
        s_past = jnp.where(mask_past, s_past, NEG)
        s_new = jnp.where(mask_new, s_new, NEG)
        m = jnp.maximum(jnp.max(s_past, axis=-1, keepdims=True), jnp.max(s_new, axis=-1, keepdims=True))
        e_past = jnp.where(mask_past, jnp.exp(s_past - m), 0.0)
        e_new = jnp.where(mask_new, jnp.exp(s_new - m), 0.0)
        denom = jnp.sum(e_past, axis=-1, keepdims=True) + jnp.sum(e_new, axis=-1, keepdims=True)
        return (_dot(e_past.astype(BF16), v_past) + _dot(e_new.astype(BF16), v_new)) / denom

    kpos = lax.broadcasted_iota(jnp.int32, (1, past), 1)
    dist_s = (tpos - kpos).astype(F32)
    mask_s = _dot(sel.astype(BF16), ex_ref[...]) > 0.5
    s_past = _dot_nt(q, ks_ref[...].astype(BF16)) * SCALE - slope * dist_s
    s_new = _dot_nt(q, ksn_ref[...].astype(BF16)) * SCALE - slope * dist_new_f
    new_blk = past // SEL_BLOCK
    sel_new = jnp.sum(jnp.where(lax.broadcasted_iota(jnp.int32, (1, sel_lanes), 1) == new_blk, sel, 0.0),
                      axis=-1, keepdims=True) > 0.5
    mask_sn = sel_new & mask_new
    s_new = jnp.where(mask_sn, s_new, NEG)
    s_past = jnp.where(mask_s, s_past, NEG)
    m = jnp.maximum(jnp.max(s_past, axis=-1, keepdims=True), jnp.max(s_new, axis=-1, keepdims=True))
    e_past = jnp.where(mask_s, jnp.exp(s_past - m), 0.0)
    e_new = jnp.where(mask_sn, jnp.exp(s_new - m), 0.0)
    denom = jnp.sum(e_past, axis=-1, keepdims=True) + jnp.sum(e_new, axis=-1, keepdims=True)
    o_s = (_dot(e_past.astype(BF16), vs_ref[...].astype(BF16))
           + _dot(e_new.astype(BF16), vsn_ref[...].astype(BF16))) / denom

    win_buf = kw_ref.shape[0]
    wpos = (past - win_buf) + lax.broadcasted_iota(jnp.int32, (1, win_buf), 1)
    dist_w = tpos - wpos
    mask_w = (dist_w >= 0) & (dist_w < WINDOW)
    s_wp = _dot_nt(q, kw_ref[...].astype(BF16)) * SCALE - slope * dist_w.astype(F32)
    s_wn = _dot_nt(q, kwn_ref[...].astype(BF16)) * SCALE - slope * dist_new_f
    o_w = two_part_attention(s_wp, mask_w, vw_ref[...].astype(BF16), s_wn, vwn_ref[...].astype(BF16))

    gates = jax.nn.sigmoid(gz_ref[...])
    o_ref[...] = gates[:, 0:1] * o_c + gates[:, 1:2] * o_s + gates[:, 2:3] * o_w


def _nsa_sample_attention(q, comp, slc, slc_new, win, win_new, gz, slopes, tnew):
    batch = q.shape[0]
    rows = q.shape[2]
    past = slc.shape[2]
    total = past + tnew
    n_c = (total - CMP_LEN) // CMP_STRIDE + 1
    n_s = -(-total // SEL_BLOCK)
    k_sel = min(N_SEL, n_s)
    rows_c = comp.shape[2]
    assert n_c <= rows_c and (n_c - 1) * CMP_STRIDE + CMP_LEN <= past
    sel_lanes = 128 * (-(-n_s // 128))
    new_lanes = slc_new.shape[2]
    win_buf = win.shape[1]
    ov = _overlap_matrix(n_c, n_s, rows_c, sel_lanes)
    ex = jnp.asarray(np.arange(sel_lanes)[:, None] == (np.arange(past)[None, :] // SEL_BLOCK), dtype=BF16)

    def bg4(shape):
        return pl.BlockSpec((None, None) + shape, lambda b, g, sl: (b, g, 0, 0))

    def bgv4(shape):
        return pl.BlockSpec((None, None) + shape, lambda b, g, sl: (b, N_KV + g, 0, 0))

    def cmp_spec(c):
        return pl.BlockSpec((None, None, rows_c, HEAD_DIM), lambda b, g, sl: (c, b * N_KV + g, 0, 0))

    grid_spec = pltpu.PrefetchScalarGridSpec(
        num_scalar_prefetch=1,
        grid=(batch, N_KV),
        in_specs=[bg4((rows, HEAD_DIM)), cmp_spec(0), cmp_spec(1),
                  bg4((past, HEAD_DIM)), bgv4((past, HEAD_DIM)),
                  bg4((new_lanes, HEAD_DIM)), bgv4((new_lanes, HEAD_DIM)),
                  pl.BlockSpec((None, win_buf, HEAD_DIM), lambda b, g, sl: (b, 0, g)),
                  pl.BlockSpec((None, win_buf, HEAD_DIM), lambda b, g, sl: (b, 0, N_KV + g)),
                  bg4((new_lanes, HEAD_DIM)), bgv4((new_lanes, HEAD_DIM)),
                  bg4((rows, GATE_LANES)),
                  pl.BlockSpec((rows_c, sel_lanes), lambda b, g, sl: (0, 0)),
                  pl.BlockSpec((sel_lanes, past), lambda b, g, sl: (0, 0))],
        out_specs=bg4((rows, HEAD_DIM)))
    return pl.pallas_call(
        functools.partial(_nsa_sample_kernel, past=past, tnew=tnew, n_c=n_c, n_s=n_s, k_sel=k_sel),
        grid_spec=grid_spec,
        out_shape=jax.ShapeDtypeStruct((batch, N_KV, rows, HEAD_DIM), F32),
        compiler_params=pltpu.CompilerParams(
            dimension_semantics=("parallel", "arbitrary"),
            vmem_limit_bytes=V7X_VMEM_LIMIT),
        name="nsa_sample",
    )(slopes, q, comp, comp, slc, slc, slc_new, slc_new, win, win, win_new, win_new, gz, ov, ex)


GATHER_PAGES_PER_STEP = 8


def _gather_kernel(pt_ref, *refs):
    page_refs, o_ref = refs[:-1], refs[-1]
    for k, page_ref in enumerate(page_refs):
        for c in range(2):
            for g in range(N_KV):
                o_ref[c * N_KV + g, k * PAGE_SIZE:(k + 1) * PAGE_SIZE, :] = page_ref[:, c, g, :]


def _gather_pages(pool, page_table, layer):
    batch, n_pages = page_table.shape
    pps = GATHER_PAGES_PER_STEP
    assert n_pages % pps == 0
    planes = 2 * N_KV

    def page_spec(k):
        return pl.BlockSpec((None, None, PAGE_SIZE, 2, N_KV, HEAD_DIM),
                            lambda b, p, pt: (layer, pt[b, p * pps + k], 0, 0, 0, 0))

    grid_spec = pltpu.PrefetchScalarGridSpec(
        num_scalar_prefetch=1,
        grid=(batch, n_pages // pps),
        in_specs=[page_spec(k) for k in range(pps)],
        out_specs=pl.BlockSpec((None, planes, pps * PAGE_SIZE, HEAD_DIM), lambda b, p, pt: (b, 0, p, 0)))
    return pl.pallas_call(
        _gather_kernel,
        grid_spec=grid_spec,
        out_shape=jax.ShapeDtypeStruct((batch, planes, n_pages * PAGE_SIZE, HEAD_DIM), pool.dtype),
        name="gather_pages",
    )(page_table, *([pool] * pps))


def _conv_kernel(zb_ref, zc_ref, zu_ref, prev_ref, ck_ref, y_ref, last_ref, *, t_real):
    v = zc_ref[...] * zu_ref[...]
    t = v.shape[0]
    row = lax.broadcasted_iota(jnp.int32, (t, 1), 0)
    v_m1 = jnp.where(row == 0, prev_ref[1:2, :], pltpu.roll(v, 1, 0))
    v_m2 = jnp.where(row == 0, prev_ref[0:1, :], jnp.where(row == 1, prev_ref[1:2, :], pltpu.roll(v, 2, 0)))
    y = ck_ref[0:1, :] * v_m2 + ck_ref[1:2, :] * v_m1 + ck_ref[2:3, :] * v
    y_ref[...] = (zb_ref[...] * y).astype(y_ref.dtype)
    last_ref[...] = v[t_real - (CONV_W - 1):t_real]


def _conv_mix(z, prev, conv_k, out_dtype, t_real):
    assert CONV_W == 3 and t_real >= CONV_W - 1
    n_seq, t, d3 = z.shape
    d = d3 // 3
    tc = 512
    nc = d // tc
    return pl.pallas_call(
        functools.partial(_conv_kernel, t_real=t_real),
        grid=(n_seq, nc),
        in_specs=[pl.BlockSpec((None, t, tc), lambda s, c: (s, 0, c)),
                  pl.BlockSpec((None, t, tc), lambda s, c: (s, 0, nc + c)),
                  pl.BlockSpec((None, t, tc), lambda s, c: (s, 0, 2 * nc + c)),
                  pl.BlockSpec((None, CONV_W - 1, tc), lambda s, c: (s, 0, c)),
                  pl.BlockSpec((CONV_W, tc), lambda s, c: (0, c))],
        out_specs=[pl.BlockSpec((None, t, tc), lambda s, c: (s, 0, c)),
                   pl.BlockSpec((None, CONV_W - 1, tc), lambda s, c: (s, 0, c))],
        out_shape=[jax.ShapeDtypeStruct((n_seq, t, d), out_dtype),
                   jax.ShapeDtypeStruct((n_seq, CONV_W - 1, d), F32)],
        compiler_params=pltpu.CompilerParams(vmem_limit_bytes=V7X_VMEM_LIMIT),
        name="conv_mix",
    )(z, z, z, prev, conv_k)


N_MAIN = Q_DIM + 3 * KV_DIM


def _gate_weights(w_in):
    n_layers, d, _ = w_in.shape
    wg = w_in[:, :, N_MAIN:].reshape(n_layers, d, 3, N_KV, GROUP).transpose(0, 1, 3, 2, 4)
    wg = wg.reshape(n_layers, d, N_KV, 3 * GROUP)
    wg = jnp.pad(wg, ((0, 0), (0, 0), (0, 0), (0, GATE_LANES - 3 * GROUP)))
    return wg.reshape(n_layers, d, N_KV * GATE_LANES).astype(BF16)


def _split_kv(z, lead):
    return tuple(z[:, Q_DIM + j * KV_DIM:Q_DIM + (j + 1) * KV_DIM].reshape(lead + (2, N_KV, HEAD_DIM))
                 for j in range(3))


def _nsa_prompt_layer(x, h, w_main, w_gate, cmp_weights, w_out, layer, slopes, batch, seq, win_buf):
    z = _gemm(h, w_main)
    gz = _gemm(h, w_gate, layer)
    kv_c, kv_s, kv_w = _split_kv(z, (batch, seq))
    comp = _compress_prompt(z, batch, seq, cmp_weights, layer)
    o = _nsa_prompt_attention(z, gz, comp, slopes, batch, seq)
    x = _gemm(o, w_out, res=x)
    pad = max(win_buf - seq, 0)
    kv_w_pad = jnp.pad(kv_w, ((0, 0), (pad, 0), (0, 0), (0, 0), (0, 0)))
    return x, kv_c, kv_s, kv_w_pad[:, kv_w_pad.shape[1] - win_buf:]


def _nsa_sample_layer(x, h, w_in, w_gate, cmp_weights, w_out, layer, slopes, pools_c, pools_s, win_state,
                      page_table, batch, tnew):
    z, w_main_b = _gemm(h, w_in, layer, n=N_MAIN, tm=batch * tnew, emit_bf16=True)
    gz = _gemm(h, w_gate, layer, tm=batch * tnew)
    kv_c, kv_s, kv_w = _split_kv(z, (batch, tnew))
    cmp_dense = _gather_pages(pools_c, page_table, layer)
    slc_dense = _gather_pages(pools_s, page_table, layer)
    past = cmp_dense.shape[2]
    assert past % SEL_BLOCK == 0 and tnew <= SEL_BLOCK
    comp = _compress_planes(cmp_dense, cmp_weights, layer)

    def new_rows(kv):
        kv = kv.transpose(0, 2, 3, 1, 4).reshape(batch, 2 * N_KV, tnew, HEAD_DIM)
        return jnp.pad(kv, ((0, 0), (0, 0), (0, 128 - tnew), (0, 0)))

    q = z[:, :Q_DIM].reshape(batch, tnew, N_KV, GROUP, HEAD_DIM).transpose(0, 2, 1, 3, 4)
    q = q.reshape(batch, N_KV, tnew * GROUP, HEAD_DIM)
    gq = gz.reshape(batch, tnew, N_KV, GATE_LANES)[..., :3 * GROUP].reshape(batch, tnew, N_KV, 3, GROUP)
    gq = gq.transpose(0, 2, 1, 4, 3).reshape(batch, N_KV, tnew * GROUP, 3)
    gq = jnp.pad(gq, ((0, 0), (0, 0), (0, 0), (0, GATE_LANES - 3)))
    win_buf = win_state.shape[1]
    o = _nsa_sample_attention(q, comp, slc_dense, new_rows(kv_s),
                              win_state.reshape(batch, win_buf, KV_DIM), new_rows(kv_w), gq, slopes, tnew)
    o = o.reshape(batch, N_KV, tnew, GROUP, HEAD_DIM).transpose(0, 2, 1, 3, 4).reshape(batch * tnew, Q_DIM)
    x, w_out_b = _gemm(o.astype(BF16), w_out, layer, res=x, tm=batch * tnew, emit_bf16=True)
    kw_full = jnp.concatenate([win_state, kv_w], axis=1)
    return x, kv_c, kv_s, kw_full[:, kw_full.shape[1] - win_buf:], (w_main_b, w_out_b)


def _matmul(a, w, layer, **kw):
    if layer is None:
        return _gemm(a, w, **kw), w
    return _gemm(a, w, layer, emit_bf16=True, **kw)


def _conv_layer(x, h, prev, w_in, conv_k, w_out, layer, n_seq, t, y_dtype):
    d = x.shape[1]
    tm = min(1024, n_seq * t)
    z, w_in_b = _matmul(h, w_in, layer, tm=tm)
    z = z.reshape(n_seq, t, 3 * d)
    t_pad = -(-t // 8) * 8
    z = jnp.pad(z, ((0, 0), (0, t_pad - t), (0, 0)))
    y, last = _conv_mix(z, prev, conv_k, y_dtype, t)
    y = y[:, :t].reshape(n_seq * t, d).astype(BF16)
    x, w_out_b = _matmul(y, w_out, layer, res=x, tm=tm)
    return x, last, (w_in_b, w_out_b)


def _mlp(x, g, w_up, w_down, layer):
    m = x.shape[0]
    h = _rmsnorm(x, g, BF16)
    a, w_up_b = _matmul(h, w_up, layer, act="relu2", out_dtype=BF16, tm=min(1024, m))
    x, w_down_b = _matmul(a, w_down, layer, res=x, tm=min(1024, m), tk=4096)
    return x, (w_up_b, w_down_b)


def kernel(x_prompt, x_sample, cache_kv_cmp, cache_kv_slc, state_win_kv, state_conv, page_table, norm_mix, norm_mlp, norm_final, nsa_w_in, nsa_cmp_pe, nsa_cmp_w1, nsa_cmp_w2, nsa_w_out, conv_w_in, conv_kernel, conv_w_out, mlp_w_up, mlp_w_down):
    batch, seq, d = x_prompt.shape
    dec_batch, dec_seq, _ = x_sample.shape
    depth = norm_mix.shape[0]
    win_buf = state_win_kv.shape[2]
    slopes = _alibi_slopes()
    xp = x_prompt.reshape(batch * seq, d)
    xs = x_sample.reshape(dec_batch * dec_seq, d)
    kvc_p, kvc_s, kvs_p, kvs_s, win_p, win_s, cv_p, cv_s = [], [], [], [], [], [], [], []
    w_gate = _gate_weights(nsa_w_in)
    cmp_weights = _compress_weights(nsa_cmp_pe, nsa_cmp_w1, nsa_cmp_w2)
    for i in range(depth):
        j = i // 2
        hp = _rmsnorm(xp, norm_mix[i], BF16)
        hs = _rmsnorm(xs, norm_mix[i], BF16)
        if i % 2 == 0:
            xs, a_s, b_s, w_s, (w_main_b, w_out_b) = _nsa_sample_layer(
                xs, hs, nsa_w_in, w_gate, cmp_weights, nsa_w_out, j, slopes, cache_kv_cmp, cache_kv_slc,
                state_win_kv[j], page_table, dec_batch, dec_seq)
            xp, a_p, b_p, w_p = _nsa_prompt_layer(xp, hp, w_main_b, w_gate, cmp_weights, w_out_b, j, slopes,
                                                  batch, seq, win_buf)
            kvc_p.append(a_p); kvc_s.append(a_s)
            kvs_p.append(b_p); kvs_s.append(b_s)
            win_p.append(w_p); win_s.append(w_s)
        else:
            zeros = jnp.zeros((batch, CONV_W - 1, d), F32)
            xs, c_s, (w_in_b, w_out_b) = _conv_layer(xs, hs, state_conv[j], conv_w_in, conv_kernel[j], conv_w_out, j,
                                                     dec_batch, dec_seq, F32)
            xp, c_p, _ = _conv_layer(xp, hp, zeros, w_in_b, conv_kernel[j], w_out_b, None, batch, seq, BF16)
            cv_p.append(c_p); cv_s.append(c_s)
        xs, (w_up_b, w_down_b) = _mlp(xs, norm_mlp[i], mlp_w_up, mlp_w_down, i)
        xp, _ = _mlp(xp, norm_mlp[i], w_up_b, w_down_b, None)
    y_prompt = _rmsnorm(xp, norm_final, F32).reshape(batch, seq, d)
    y_sample = _rmsnorm(xs, norm_final, F32).reshape(dec_batch, dec_seq, d)
    return (y_prompt, y_sample, jnp.stack(kvc_p), jnp.stack(kvc_s), jnp.stack(kvs_p), jnp.stack(kvs_s),
            jnp.stack(win_p), jnp.stack(win_s), jnp.stack(cv_p), jnp.stack(cv_s))
```

```python
import functools

import numpy as np
import jax
import jax.numpy as jnp
from jax import lax
from jax.experimental import pallas as pl
from jax.experimental.pallas import tpu as pltpu

N_HEADS = 32
HEAD_DIM = 128
N_KV = 4
GROUP = N_HEADS // N_KV
CMP_LEN = 32
CMP_STRIDE = 16
SEL_BLOCK = 64
N_SEL = 16
WINDOW = 512
CONV_W = 3
PAGE_SIZE = 128
RMS_EPS = 1e-6
NEG = -1e30
BIG = 1e30
SCALE = HEAD_DIM ** -0.5
LOG2E = 1.4426950408889634
Q_DIM = N_HEADS * HEAD_DIM
KV_DIM = 2 * N_KV * HEAD_DIM
GATE_LANES = 128

V7X_VMEM_LIMIT = 60 * 1024 * 1024
BF16 = jnp.bfloat16
F32 = jnp.float32
NT_DIMS = (((1,), (1,)), ((), ()))


def _alibi_slopes():
    i = np.arange(1, N_HEADS + 1, dtype=np.float32)
    return jnp.asarray(2.0 ** (-8.0 * i / N_HEADS), dtype=jnp.float32)


def _dot(a, b):
    return jnp.dot(a, b, preferred_element_type=F32)


def _dot_nt(a, b):
    return lax.dot_general(a, b, NT_DIMS, preferred_element_type=F32)


def _dot_split3(x, m):
    hi = x.astype(BF16)
    r1 = x - hi.astype(F32)
    mid = r1.astype(BF16)
    lo = (r1 - mid.astype(F32)).astype(BF16)
    return _dot(hi, m) + _dot(mid, m) + _dot(lo, m)


def _dot_split3_nt(m, x):
    hi = x.astype(BF16)
    r1 = x - hi.astype(F32)
    mid = r1.astype(BF16)
    lo = (r1 - mid.astype(F32)).astype(BF16)
    return _dot_nt(m, hi) + _dot_nt(m, mid) + _dot_nt(m, lo)


def _rms_kernel(x_ref, g_ref, o_ref):
    x = x_ref[...]
    ms = jnp.mean(x * x, axis=-1, keepdims=True)
    o_ref[...] = (x * lax.rsqrt(ms + RMS_EPS) * g_ref[...]).astype(o_ref.dtype)


def _rmsnorm(x, g, out_dtype):
    m, d = x.shape
    tm = min(m, 256)
    return pl.pallas_call(
        _rms_kernel,
        grid=(m // tm,),
        in_specs=[pl.BlockSpec((tm, d), lambda i: (i, 0)),
                  pl.BlockSpec((1, d), lambda i: (0, 0))],
        out_specs=pl.BlockSpec((tm, d), lambda i: (i, 0)),
        out_shape=jax.ShapeDtypeStruct((m, d), out_dtype),
        name="rmsnorm",
    )(x, g.reshape(1, d))


def _gemm_kernel(*refs, nk, act, has_res, emit_bf16, trans_w):
    a_ref, w_ref = refs[0], refs[1]
    res_ref = refs[2] if has_res else None
    o_ref = refs[3] if has_res else refs[2]
    w = w_ref[...]
    if emit_bf16:
        w = w.astype(BF16)
        refs[-1][...] = w
    dot = _dot_nt if trans_w else _dot

    def finish(acc):
        if act == "relu2":
            acc = jnp.maximum(acc, 0.0)
            acc = acc * acc
        if has_res:
            acc = acc + res_ref[...]
        o_ref[...] = acc.astype(o_ref.dtype)

    if nk == 1:
        finish(dot(a_ref[...], w))
        return

    k = pl.program_id(2)
    part = dot(a_ref[...], w)

    @pl.when(k == 0)
    def _():
        o_ref[...] = part + res_ref[...] if has_res else part

    @pl.when(k > 0)
    def _():
        o_ref[...] += part


def _gemm(a, w, layer=None, *, n=None, res=None, act=None, out_dtype=F32, tm=1024, tn=1024, tk=None,
          emit_bf16=False, trans_w=False):
    m, kdim = a.shape
    n = w.shape[-2 if trans_w else -1] if n is None else n
    tm, tn = min(tm, m), min(tn, n)
    tk = kdim if tk is None else min(tk, kdim)
    assert m % tm == 0 and n % tn == 0 and kdim % tk == 0
    nk = kdim // tk
    assert nk == 1 or (act is None and out_dtype == F32)
    assert not emit_bf16 or m == tm
    w_block = (tn, tk) if trans_w else (tk, tn)

    def w_index(i, j, k):
        return (j, k) if trans_w else (k, j)

    if layer is None:
        w_spec = pl.BlockSpec(w_block, w_index)
    else:
        w_spec = pl.BlockSpec((None,) + w_block, lambda i, j, k: (layer,) + w_index(i, j, k))
    in_specs = [pl.BlockSpec((tm, tk), lambda i, j, k: (i, k)), w_spec]
    args = [a, w]
    if res is not None:
        in_specs.append(pl.BlockSpec((tm, tn), lambda i, j, k: (i, j)))
        args.append(res)
    out_specs = pl.BlockSpec((tm, tn), lambda i, j, k: (i, j))
    out_shape = jax.ShapeDtypeStruct((m, n), out_dtype)
    if emit_bf16:
        out_specs = [out_specs, pl.BlockSpec(w_block, w_index)]
        out_shape = [out_shape, jax.ShapeDtypeStruct((n, kdim) if trans_w else (kdim, n), BF16)]
    return pl.pallas_call(
        functools.partial(_gemm_kernel, nk=nk, act=act, has_res=res is not None, emit_bf16=emit_bf16,
                          trans_w=trans_w),
        grid=(m // tm, n // tn, nk),
        in_specs=in_specs,
        out_specs=out_specs,
        out_shape=out_shape,
        compiler_params=pltpu.CompilerParams(
            dimension_semantics=("parallel", "parallel", "arbitrary"),
            vmem_limit_bytes=V7X_VMEM_LIMIT),
        name="gemm",
    )(*args)


def _compress_kernel(*refs):
    x_refs, (pe_ref, w1_ref, w2_ref, o_ref) = refs[:-4], refs[-4:]
    planes = len(x_refs)
    n_chunks = x_refs[0].shape[0] // CMP_STRIDE
    half = CMP_STRIDE * HEAD_DIM
    x = jnp.concatenate(
        [jnp.concatenate([x_ref[pl.ds(l, n_chunks, stride=CMP_STRIDE), :] for l in range(CMP_STRIDE)], axis=1)
         for x_ref in x_refs], axis=0)
    lo = _dot((x + pe_ref[0:1, :]).astype(BF16), w1_ref[0:half, :])
    hi = _dot((x + pe_ref[1:2, :]).astype(BF16), w1_ref[half:2 * half, :])
    hid = jax.nn.gelu(lo + pltpu.roll(hi, planes * n_chunks - 1, 0))
    out = _dot(hid.astype(BF16), w2_ref[...])
    for p in range(planes):
        o_ref[p] = out[p * n_chunks:(p + 1) * n_chunks]


def _compress_weights(pe, w1, w2):
    n_layers = pe.shape[0]
    half = CMP_STRIDE * HEAD_DIM
    pe2 = pe.reshape(n_layers, 2, CMP_LEN // CMP_STRIDE, half)
    w1f = w1.reshape(n_layers, 2, CMP_LEN * HEAD_DIM, w1.shape[-1]).astype(BF16)
    return pe2, w1f, w2.astype(BF16)


def _compress(x, x_specs, grid, out_index, t, n_seq, cmp_weights, layer):
    pe2, w1f, w2b = cmp_weights
    half = CMP_STRIDE * HEAD_DIM
    hidden = w1f.shape[-1]
    n_chunks = t // CMP_STRIDE
    planes = len(x_specs)

    def weight_spec(shape):
        return pl.BlockSpec((None, None) + shape, lambda c, *_: (layer, c, 0, 0))

    return pl.pallas_call(
        _compress_kernel,
        grid=(2,) + grid,
        in_specs=x_specs + [weight_spec((2, half)), weight_spec((2 * half, hidden)), weight_spec((hidden, HEAD_DIM))],
        out_specs=pl.BlockSpec((None, planes, n_chunks, HEAD_DIM), lambda c, *i: (c, out_index(*i), 0, 0)),
        out_shape=jax.ShapeDtypeStruct((2, n_seq, n_chunks, HEAD_DIM), F32),
        compiler_params=pltpu.CompilerParams(vmem_limit_bytes=V7X_VMEM_LIMIT),
        name="compress",
    )(*([x] * planes), pe2, w1f, w2b)


def _compress_prompt(z, batch, seq, cmp_weights, layer):
    kcol = Q_DIM // HEAD_DIM

    def plane_spec(g):
        return pl.BlockSpec((seq, HEAD_DIM), lambda c, b: (b, kcol + c * N_KV + g))

    return _compress(z, [plane_spec(g) for g in range(N_KV)], (batch,), lambda b: b, seq, batch * N_KV,
                     cmp_weights, layer)


def _compress_planes(x, cmp_weights, layer):
    batch, _, t, _ = x.shape
    x_spec = pl.BlockSpec((None, None, t, HEAD_DIM), lambda c, b, g: (b, c * N_KV + g, 0, 0))
    return _compress(x, [x_spec], (batch, N_KV), lambda b, g: b * N_KV + g, t, batch * N_KV, cmp_weights, layer)


def _softmax_rows(s, mask):
    s = jnp.where(mask, s, NEG)
    m = jnp.max(s, axis=-1, keepdims=True)
    e = jnp.exp(s - m)
    return jnp.where(mask, e / jnp.sum(e, axis=-1, keepdims=True), 0.0)


def _select_blocks(imp, blk_t, n_s, k_sel):
    lanes = imp.shape[-1]
    s_idx = lax.broadcasted_iota(jnp.int32, (1, lanes), 1)
    forced = (s_idx == 0) | (s_idx == blk_t) | (s_idx == blk_t - 1)
    imp = jnp.where(s_idx <= blk_t, jnp.where(forced, BIG, imp), NEG)
    imp = jnp.where(s_idx < n_s, imp, -jnp.inf)
    s_f = s_idx.astype(F32)
    sel = jnp.zeros(imp.shape, F32)
    for _ in range(k_sel):
        top = jnp.max(imp, axis=-1, keepdims=True)
        first = jnp.min(jnp.where(imp == top, s_f, float(lanes)), axis=-1, keepdims=True)
        hit = s_f == first
        sel = jnp.where(hit, 1.0, sel)
        imp = jnp.where(hit, -jnp.inf, imp)
    return sel


def _select_blocks_t(imp_t, blk_t, n_s, k_sel):
    rows = imp_t.shape[0]
    s_idx = lax.broadcasted_iota(jnp.int32, (rows, 1), 0)
    forced = (s_idx == 0) | (s_idx == blk_t) | (s_idx == blk_t - 1)
    imp = jnp.where(s_idx <= blk_t, jnp.where(forced, BIG, imp_t), NEG)
    imp = jnp.where(s_idx < n_s, imp, -jnp.inf)
    rank = jnp.zeros(imp.shape, F32)
    for other in range(n_s):
        row = imp[other:other + 1, :]
        wins_tie = jnp.where(s_idx > other, 1.0, 0.0)
        rank = rank + jnp.where(row > imp, 1.0, jnp.where(row == imp, wins_tie, 0.0))
    return jnp.where(rank < k_sel, 1.0, 0.0)


def _overlap_matrix(n_c, n_s, rows, cols):
    ci = np.arange(rows)[:, None]
    sj = np.arange(cols)[None, :]
    ov = ((ci * CMP_STRIDE <= sj * SEL_BLOCK + SEL_BLOCK - 1)
          & (ci * CMP_STRIDE + CMP_LEN - 1 >= sj * SEL_BLOCK)
          & (ci < n_c) & (sj < n_s))
    return jnp.asarray(ov, dtype=BF16)


def _nsa_prompt_kernel(sl_ref, q_ref, ks_ref, vs_ref, kw_ref, vw_ref, kc_ref, vc_ref, gz_ref, ovt_ref,
                       o_ref, qs_sc, p_sc, oc_sc, m_sc, acc_sc, *, seq, tq, n_c, n_s, k_sel, wslab):
    g = pl.program_id(1)
    qi = pl.program_id(2)
    t0 = qi * tq
    tpos = t0 + lax.broadcasted_iota(jnp.int32, (tq, 1), 0)
    q_scaled = q_ref[...] * (SCALE * LOG2E)
    for r in range(GROUP):
        qs_sc[r * tq:(r + 1) * tq, 0:HEAD_DIM] = q_scaled[:, r * HEAD_DIM:(r + 1) * HEAD_DIM].astype(BF16)
    slopes2 = [sl_ref[g * GROUP + r] * LOG2E for r in range(GROUP)]
    ones = jnp.ones((tq, HEAD_DIM), BF16)

    def head(x, r):
        return x[r * tq:(r + 1) * tq]

    lanes_c = kc_ref.shape[0]
    n_idx = lax.broadcasted_iota(jnp.int32, (1, lanes_c), 1)
    cpos = n_idx * CMP_STRIDE + (CMP_LEN - 1)
    mask_c = (n_idx < n_c) & (tpos >= cpos)
    cpos_f = cpos.astype(F32)
    s_all = _dot_nt(qs_sc[:, 0:HEAD_DIM], kc_ref[...].astype(BF16))
    psum = jnp.zeros((tq, lanes_c), F32)
    for r in range(GROUP):
        s = jnp.where(mask_c, head(s_all, r) + slopes2[r] * cpos_f, NEG)
        e = jnp.exp2(s - jnp.max(s, axis=-1, keepdims=True))
        p = jnp.where(mask_c, e / jnp.sum(e, axis=-1, keepdims=True), 0.0)
        p_sc[r * tq:(r + 1) * tq, 0:lanes_c] = p.astype(BF16)
        psum = psum + p
    oc_sc[...] = _dot(p_sc[:, 0:lanes_c], vc_ref[...].astype(BF16))
    sel_lanes = ovt_ref.shape[0]
    s_rows = 8 * (-(-n_s // 8))
    imp_t = _dot_split3_nt(ovt_ref[...], psum)[0:s_rows]
    blk_t = (t0 + lax.broadcasted_iota(jnp.int32, (1, tq), 1)) // SEL_BLOCK
    sel_t = _select_blocks_t(imp_t, blk_t, n_s, k_sel)
    sel = jnp.concatenate([sel_t, jnp.zeros((sel_lanes - s_rows, tq), F32)], axis=0).T
    sel_bias = jnp.where(sel > 0.5, 0.0, NEG).astype(BF16)
    for r in range(GROUP):
        qs_sc[r * tq:(r + 1) * tq, HEAD_DIM:HEAD_DIM + sel_lanes] = sel_bias

    m_sc[...] = jnp.full(m_sc.shape, NEG, F32)
    acc_sc[...] = jnp.zeros(acc_sc.shape, F32)

    def key_tile(j, diagonal):
        k0 = pl.multiple_of(j * tq, tq)
        kpos = k0 + lax.broadcasted_iota(jnp.int32, (1, tq), 1)
        key_blk = (k0 + lax.broadcasted_iota(jnp.int32, (tq, sel_lanes), 0)) // SEL_BLOCK
        one_hot = jnp.where(lax.broadcasted_iota(jnp.int32, (tq, sel_lanes), 1) == key_blk, 1.0, 0.0)
        ka = jnp.concatenate([ks_ref[pl.ds(k0, tq), :].astype(BF16), one_hot.astype(BF16)], axis=1)
        va = jnp.concatenate([vs_ref[pl.ds(k0, tq), :].astype(BF16), ones], axis=1)
        kpos_f = kpos.astype(F32)
        causal = jnp.where(tpos >= kpos, 0.0, NEG) if diagonal else None
        s_all = _dot_nt(qs_sc[...], ka)
        alphas = []
        for r in range(GROUP):
            s = head(s_all, r) + slopes2[r] * kpos_f
            if diagonal:
                s = s + causal
            m_old = head(m_sc, r)
            m_new = jnp.maximum(m_old, jnp.broadcast_to(jnp.max(s, axis=-1, keepdims=True), m_old.shape))
            alphas.append(jnp.exp2(m_old - m_new))
            p_sc[r * tq:(r + 1) * tq, 0:tq] = jnp.exp2(s - jnp.concatenate([m_new, m_new], axis=1)).astype(BF16)
            m_sc[r * tq:(r + 1) * tq, :] = m_new
        pv = _dot(p_sc[:, 0:tq], va)
        for r in range(GROUP):
            alpha2 = jnp.concatenate([alphas[r], alphas[r]], axis=1)
            acc_sc[r * tq:(r + 1) * tq, :] = alpha2 * head(acc_sc, r) + head(pv, r)

    def full_tile(j, carry):
        key_tile(j, False)
        return carry

    lax.fori_loop(0, qi, full_tile, 0)
    key_tile(qi, True)

    w0 = jnp.clip(t0 - WINDOW, 0, seq - wslab)
    w0 = pl.multiple_of(w0, tq)
    kw = kw_ref[pl.ds(w0, wslab), :].astype(BF16)
    vwa = jnp.concatenate([vw_ref[pl.ds(w0, wslab), :].astype(BF16), jnp.ones((wslab, HEAD_DIM), BF16)], axis=1)
    wpos = w0 + lax.broadcasted_iota(jnp.int32, (1, wslab), 1)
    dist_w = tpos - wpos
    band = jnp.where((dist_w >= 0) & (dist_w < WINDOW), 0.0, NEG)
    wpos_f = wpos.astype(F32)
    s_all = _dot_nt(qs_sc[:, 0:HEAD_DIM], kw)
    for r in range(GROUP):
        s = head(s_all, r) + band + slopes2[r] * wpos_f
        p_sc[r * tq:(r + 1) * tq, :] = jnp.exp2(s - jnp.max(s, axis=-1, keepdims=True)).astype(BF16)
    ow_all = _dot(p_sc[...], vwa)
    gates = jax.nn.sigmoid(gz_ref[...])
    for r in range(GROUP):
        ow = head(ow_all, r)
        acc = head(acc_sc, r)
        o = (gates[:, r:r + 1] * head(oc_sc, r)
             + gates[:, GROUP + r:GROUP + r + 1] * (acc[:, :HEAD_DIM] / acc[:, HEAD_DIM:])
             + gates[:, 2 * GROUP + r:2 * GROUP + r + 1] * (ow[:, :HEAD_DIM] / ow[:, HEAD_DIM:]))
        o_ref[:, r * HEAD_DIM:(r + 1) * HEAD_DIM] = o.astype(o_ref.dtype)


def _nsa_prompt_attention(z, gz, comp, slopes, batch, seq):
    tq = min(256, seq)
    nq = seq // tq
    n_c = (seq - CMP_LEN) // CMP_STRIDE + 1
    n_s = -(-seq // SEL_BLOCK)
    k_sel = min(N_SEL, n_s)
    rows_c = comp.shape[2]
    sel_lanes = 128 * (-(-n_s // 128))
    wslab = min(WINDOW + tq, seq)
    assert wslab >= tq and wslab >= rows_c
    ov = _overlap_matrix(n_c, n_s, rows_c, sel_lanes)
    gw = Q_DIM // N_KV
    kcol = Q_DIM // HEAD_DIM

    def kv_spec(branch, c):
        base = kcol + (branch * 2 + c) * N_KV
        return pl.BlockSpec((seq, HEAD_DIM), lambda b, g, i, sl: (b, base + g))

    grid_spec = pltpu.PrefetchScalarGridSpec(
        num_scalar_prefetch=1,
        grid=(batch, N_KV, nq),
        in_specs=[pl.BlockSpec((tq, gw), lambda b, g, i, sl: (b * nq + i, g)),
                  kv_spec(1, 0), kv_spec(1, 1), kv_spec(2, 0), kv_spec(2, 1),
                  pl.BlockSpec((None, None, rows_c, HEAD_DIM), lambda b, g, i, sl: (0, b * N_KV + g, 0, 0)),
                  pl.BlockSpec((None, None, rows_c, HEAD_DIM), lambda b, g, i, sl: (1, b * N_KV + g, 0, 0)),
                  pl.BlockSpec((tq, GATE_LANES), lambda b, g, i, sl: (b * nq + i, g)),
                  pl.BlockSpec((sel_lanes, rows_c), lambda b, g, i, sl: (0, 0))],
        out_specs=pl.BlockSpec((tq, gw), lambda b, g, i, sl: (b * nq + i, g)),
        scratch_shapes=[pltpu.VMEM((GROUP * tq, HEAD_DIM + sel_lanes), BF16),
                        pltpu.VMEM((GROUP * tq, wslab), BF16),
                        pltpu.VMEM((GROUP * tq, HEAD_DIM), F32),
                        pltpu.VMEM((GROUP * tq, HEAD_DIM), F32),
                        pltpu.VMEM((GROUP * tq, 2 * HEAD_DIM), F32)])
    return pl.pallas_call(
        functools.partial(_nsa_prompt_kernel, seq=seq, tq=tq, n_c=n_c, n_s=n_s, k_sel=k_sel, wslab=wslab),
        grid_spec=grid_spec,
        out_shape=jax.ShapeDtypeStruct((batch * seq, Q_DIM), BF16),
        compiler_params=pltpu.CompilerParams(
            dimension_semantics=("parallel", "parallel", "arbitrary"),
            vmem_limit_bytes=V7X_VMEM_LIMIT),
        name="nsa_prompt",
    )(slopes, z, z, z, z, z, comp, comp, gz, ov.T)


def _nsa_sample_kernel(sl_ref, q_ref, kc_ref, vc_ref, ks_ref, vs_ref, ksn_ref, vsn_ref, kw_ref, vw_ref,
                       kwn_ref, vwn_ref, gz_ref, ov_ref, ex_ref, o_ref, *, past, tnew, n_c, n_s, k_sel):
    g = pl.program_id(1)
    rows = tnew * GROUP
    row = lax.broadcasted_iota(jnp.int32, (rows, 1), 0)
    t_new = row // GROUP
    tpos = past + t_new
    slope = jnp.zeros((rows, 1), F32)
    for r in range(GROUP):
        slope = jnp.where(row % GROUP == r, sl_ref[g * GROUP + r], slope)
    q = q_ref[...].astype(BF16)
    new_lanes = ksn_ref.shape[0]
    i_new = lax.broadcasted_iota(jnp.int32, (1, new_lanes), 1)
    dist_new = t_new - i_new
    mask_new = (i_new < tnew) & (dist_new >= 0)
    dist_new_f = dist_new.astype(F32)

    lanes_c = kc_ref.shape[0]
    n_idx = lax.broadcasted_iota(jnp.int32, (1, lanes_c), 1)
    dist_c = tpos - (n_idx * CMP_STRIDE + (CMP_LEN - 1))
    mask_c = (n_idx < n_c) & (dist_c >= 0)
    s = _dot_nt(q, kc_ref[...].astype(BF16)) * SCALE - slope * dist_c.astype(F32)
    p_c = _softmax_rows(s, mask_c)
    o_c = _dot(p_c.astype(BF16), vc_ref[...].astype(BF16))
    imp = _dot_split3(p_c, ov_ref[...])
    sel_lanes = imp.shape[1]
    imp = jnp.sum(imp.reshape(tnew, GROUP, sel_lanes), axis=1, keepdims=True)
    imp = jnp.broadcast_to(imp, (tnew, GROUP, sel_lanes)).reshape(rows, sel_lanes)
    sel = _select_blocks(imp, tpos // SEL_BLOCK, n_s, k_sel)

    def two_part_attention(s_past, mask_past, v_past, s_new, v_new):
        s_past = jnp.where(mask_past, s_past, NEG)
        s_new = jnp.where(mask_new, s_new, NEG)
        m = jnp.maximum(jnp.max(s_past, axis=-1, keepdims=True), jnp.max(s_new, axis=-1, keepdims=True))
        e_past = jnp.where(mask_past, jnp.exp(s_past - m), 0.0)
        e_new = jnp.where(mask_new, jnp.exp(s_new - m), 0.0)
        denom = jnp.sum(e_past, axis=-1, keepdims=True) + jnp.sum(e_new, axis=-1, keepdims=True)
        return (_dot(e_past.astype(BF16), v_past) + _dot(e_new.astype(BF16), v_new)) / denom

    kpos = lax.broadcasted_iota(jnp.int32, (1, past), 1)
    dist_s = (tpos - kpos).astype(F32)
    mask_s = _dot(sel.astype(BF16), ex_ref[...]) > 0.5
    s_past = _dot_nt(q, ks_ref[...].astype(BF16)) * SCALE - slope * dist_s
    s_new = _dot_nt(q, ksn_ref[...].astype(BF16)) * SCALE - slope * dist_new_f
    new_blk = past // SEL_BLOCK
    sel_new = jnp.sum(jnp.where(lax.broadcasted_iota(jnp.int32, (1, sel_lanes), 1) == new_blk, sel, 0.0),
                      axis=-1, keepdims=True) > 0.5
    mask_sn = sel_new & mask_new
    s_new = jnp.where(mask_sn, s_new, NEG)
    s_past = jnp.where(mask_s, s_past, NEG)
    m = jnp.maximum(jnp.max(s_past, axis=-1, keepdims=True), jnp.max(s_new, axis=-1, keepdims=True))
    e_past = jnp.where(mask_s, jnp.exp(s_past - m), 0.0)
    e_new = jnp.where(mask_sn, jnp.exp(s_new - m), 0.0)
    denom = jnp.sum(e_past, axis=-1, keepdims=True) + jnp.sum(e_new, axis=-1, keepdims=True)
    o_s = (_dot(e_past.astype(BF16), vs_ref[...].astype(BF16))
           + _dot(e_new.astype(BF16), vsn_ref[...].astype(BF16))) / denom

    win_buf = kw_ref.shape[0]
    wpos = (past - win_buf) + lax.broadcasted_iota(jnp.int32, (1, win_buf), 1)
    dist_w = tpos - wpos
    mask_w = (dist_w >= 0) & (dist_w < WINDOW)
    s_wp = _dot_nt(q, kw_ref[...].astype(BF16)) * SCALE - slope * dist_w.astype(F32)
    s_wn = _dot_nt(q, kwn_ref[...].astype(BF16)) * SCALE - slope * dist_new_f
    o_w = two_part_attention(s_wp, mask_w, vw_ref[...].astype(BF16), s_wn, vwn_ref[...].astype(BF16))

    gates = jax.nn.sigmoid(gz_ref[...])
    o_ref[...] = gates[:, 0:1] * o_c + gates[:, 1:2] * o_s + gates[:, 2:3] * o_w


def _nsa_sample_attention(q, comp, slc, slc_new, win, win_new, gz, slopes, tnew):
    batch = q.shape[0]
    rows = q.shape[2]
    past = slc.shape[2]
    total = past + tnew
    n_c = (total - CMP_LEN) // CMP_STRIDE + 1
    n_s = -(-total // SEL_BLOCK)
    k_sel = min(N_SEL, n_s)
    rows_c = comp.shape[2]
    assert n_c <= rows_c and (n_c - 1) * CMP_STRIDE + CMP_LEN <= past
    sel_lanes = 128 * (-(-n_s // 128))
    new_lanes = slc_new.shape[2]
    win_buf = win.shape[1]
    ov = _overlap_matrix(n_c, n_s, rows_c, sel_lanes)
    ex = jnp.asarray(np.arange(sel_lanes)[:, None] == (np.arange(past)[None, :] // SEL_BLOCK), dtype=BF16)

    def bg4(shape):
        return pl.BlockSpec((None, None) + shape, lambda b, g, sl: (b, g, 0, 0))

    def bgv4(shape):
        return pl.BlockSpec((None, None) + shape, lambda b, g, sl: (b, N_KV + g, 0, 0))

    def cmp_spec(c):
        return pl.BlockSpec((None, None, rows_c, HEAD_DIM), lambda b, g, sl: (c, b * N_KV + g, 0, 0))

    grid_spec = pltpu.PrefetchScalarGridSpec(
        num_scalar_prefetch=1,
        grid=(batch, N_KV),
        in_specs=[bg4((rows, HEAD_DIM)), cmp_spec(0), cmp_spec(1),
                  bg4((past, HEAD_DIM)), bgv4((past, HEAD_DIM)),
                  bg4((new_lanes, HEAD_DIM)), bgv4((new_lanes, HEAD_DIM)),
                  pl.BlockSpec((None, win_buf, HEAD_DIM), lambda b, g, sl: (b, 0, g)),
                  pl.BlockSpec((None, win_buf, HEAD_DIM), lambda b, g, sl: (b, 0, N_KV + g)),
                  bg4((new_lanes, HEAD_DIM)), bgv4((new_lanes, HEAD_DIM)),
                  bg4((rows, GATE_LANES)),
                  pl.BlockSpec((rows_c, sel_lanes), lambda b, g, sl: (0, 0)),
                  pl.BlockSpec((sel_lanes, past), lambda b, g, sl: (0, 0))],
        out_specs=bg4((rows, HEAD_DIM)))
    return pl.pallas_call(
        functools.partial(_nsa_sample_kernel, past=past, tnew=tnew, n_c=n_c, n_s=n_s, k_sel=k_sel),
        grid_spec=grid_spec,
        out_shape=jax.ShapeDtypeStruct((batch, N_KV, rows, HEAD_DIM), F32),
        compiler_params=pltpu.CompilerParams(
            dimension_semantics=("parallel", "arbitrary"),
            vmem_limit_bytes=V7X_VMEM_LIMIT),
        name="nsa_sample",
    )(slopes, q, comp, comp, slc, slc, slc_new, slc_new, win, win, win_new, win_new, gz, ov, ex)


GATHER_PAGES_PER_STEP = 8


def _gather_kernel(pt_ref, *refs):
    page_refs, o_ref = refs[:-1], refs[-1]
    for k, page_ref in enumerate(page_refs):
        for c in range(2):
            for g in range(N_KV):
                o_ref[c * N_KV + g, k * PAGE_SIZE:(k + 1) * PAGE_SIZE, :] = page_ref[:, c, g, :]


def _gather_pages(pool, page_table, layer):
    batch, n_pages = page_table.shape
    pps = GATHER_PAGES_PER_STEP
    assert n_pages % pps == 0
    planes = 2 * N_KV

    def page_spec(k):
        return pl.BlockSpec((None, None, PAGE_SIZE, 2, N_KV, HEAD_DIM),
                            lambda b, p, pt: (layer, pt[b, p * pps + k], 0, 0, 0, 0))

    grid_spec = pltpu.PrefetchScalarGridSpec(
        num_scalar_prefetch=1,
        grid=(batch, n_pages // pps),
        in_specs=[page_spec(k) for k in range(pps)],
        out_specs=pl.BlockSpec((None, planes, pps * PAGE_SIZE, HEAD_DIM), lambda b, p, pt: (b, 0, p, 0)))
    return pl.pallas_call(
        _gather_kernel,
        grid_spec=grid_spec,
        out_shape=jax.ShapeDtypeStruct((batch, planes, n_pages * PAGE_SIZE, HEAD_DIM), pool.dtype),
        name="gather_pages",
    )(page_table, *([pool] * pps))


def _conv_kernel(zb_ref, zc_ref, zu_ref, prev_ref, ck_ref, y_ref, last_ref, *, t_real):
    v = zc_ref[...] * zu_ref[...]
    t = v.shape[0]
    row = lax.broadcasted_iota(jnp.int32, (t, 1), 0)
    v_m1 = jnp.where(row == 0, prev_ref[1:2, :], pltpu.roll(v, 1, 0))
    v_m2 = jnp.where(row == 0, prev_ref[0:1, :], jnp.where(row == 1, prev_ref[1:2, :], pltpu.roll(v, 2, 0)))
    y = ck_ref[0:1, :] * v_m2 + ck_ref[1:2, :] * v_m1 + ck_ref[2:3, :] * v
    y_ref[...] = (zb_ref[...] * y).astype(y_ref.dtype)
    last_ref[...] = v[t_real - (CONV_W - 1):t_real]


def _conv_mix(z, prev, conv_k, out_dtype, t_real):
    assert CONV_W == 3 and t_real >= CONV_W - 1
    n_seq, t, d3 = z.shape
    d = d3 // 3
    tc = 512
    nc = d // tc
    return pl.pallas_call(
        functools.partial(_conv_kernel, t_real=t_real),
        grid=(n_seq, nc),
        in_specs=[pl.BlockSpec((None, t, tc), lambda s, c: (s, 0, c)),
                  pl.BlockSpec((None, t, tc), lambda s, c: (s, 0, nc + c)),
                  pl.BlockSpec((None, t, tc), lambda s, c: (s, 0, 2 * nc + c)),
                  pl.BlockSpec((None, CONV_W - 1, tc), lambda s, c: (s, 0, c)),
                  pl.BlockSpec((CONV_W, tc), lambda s, c: (0, c))],
        out_specs=[pl.BlockSpec((None, t, tc), lambda s, c: (s, 0, c)),
                   pl.BlockSpec((None, CONV_W - 1, tc), lambda s, c: (s, 0, c))],
        out_shape=[jax.ShapeDtypeStruct((n_seq, t, d), out_dtype),
                   jax.ShapeDtypeStruct((n_seq, CONV_W - 1, d), F32)],
        compiler_params=pltpu.CompilerParams(vmem_limit_bytes=V7X_VMEM_LIMIT),
        name="conv_mix",
    )(z, z, z, prev, conv_k)


N_MAIN = Q_DIM + 3 * KV_DIM


def _gate_weights(w_in):
    n_layers, d, _ = w_in.shape
    wg = w_in[:, :, N_MAIN:].reshape(n_layers, d, 3, N_KV, GROUP).transpose(0, 1, 3, 2, 4)
    wg = wg.reshape(n_layers, d, N_KV, 3 * GROUP)
    wg = jnp.pad(wg, ((0, 0), (0, 0), (0, 0), (0, GATE_LANES - 3 * GROUP)))
    return wg.reshape(n_layers, d, N_KV * GATE_LANES).astype(BF16)


def _split_kv(z, lead):
    return tuple(z[:, Q_DIM + j * KV_DIM:Q_DIM + (j + 1) * KV_DIM].reshape(lead + (2, N_KV, HEAD_DIM))
                 for j in range(3))


def _nsa_prompt_layer(x, h, w_main, w_gate, cmp_weights, w_out, layer, slopes, batch, seq, win_buf):
    z = _gemm(h, w_main, trans_w=True)
    gz = _gemm(h, w_gate, layer)
    kv_c, kv_s, kv_w = _split_kv(z, (batch, seq))
    comp = _compress_prompt(z, batch, seq, cmp_weights, layer)
    o = _nsa_prompt_attention(z, gz, comp, slopes, batch, seq)
    x = _gemm(o, w_out, res=x)
    pad = max(win_buf - seq, 0)
    kv_w_pad = jnp.pad(kv_w, ((0, 0), (pad, 0), (0, 0), (0, 0), (0, 0)))
    return x, kv_c, kv_s, kv_w_pad[:, kv_w_pad.shape[1] - win_buf:]


def _nsa_sample_layer(x, h, w_in_t, w_gate, cmp_weights, w_out, layer, slopes, pools_c, pools_s, win_state,
                      page_table, batch, tnew):
    z, w_main_b = _gemm(h, w_in_t, layer, n=N_MAIN, tm=batch * tnew, emit_bf16=True, trans_w=True)
    gz = _gemm(h, w_gate, layer, tm=batch * tnew)
    kv_c, kv_s, kv_w = _split_kv(z, (batch, tnew))
    cmp_dense = _gather_pages(pools_c, page_table, layer)
    slc_dense = _gather_pages(pools_s, page_table, layer)
    past = cmp_dense.shape[2]
    assert past % SEL_BLOCK == 0 and tnew <= SEL_BLOCK
    comp = _compress_planes(cmp_dense, cmp_weights, layer)

    def new_rows(kv):
        kv = kv.transpose(0, 2, 3, 1, 4).reshape(batch, 2 * N_KV, tnew, HEAD_DIM)
        return jnp.pad(kv, ((0, 0), (0, 0), (0, 128 - tnew), (0, 0)))

    q = z[:, :Q_DIM].reshape(batch, tnew, N_KV, GROUP, HEAD_DIM).transpose(0, 2, 1, 3, 4)
    q = q.reshape(batch, N_KV, tnew * GROUP, HEAD_DIM)
    gq = gz.reshape(batch, tnew, N_KV, GATE_LANES)[..., :3 * GROUP].reshape(batch, tnew, N_KV, 3, GROUP)
    gq = gq.transpose(0, 2, 1, 4, 3).reshape(batch, N_KV, tnew * GROUP, 3)
    gq = jnp.pad(gq, ((0, 0), (0, 0), (0, 0), (0, GATE_LANES - 3)))
    win_buf = win_state.shape[1]
    o = _nsa_sample_attention(q, comp, slc_dense, new_rows(kv_s),
                              win_state.reshape(batch, win_buf, KV_DIM), new_rows(kv_w), gq, slopes, tnew)
    o = o.reshape(batch, N_KV, tnew, GROUP, HEAD_DIM).transpose(0, 2, 1, 3, 4).reshape(batch * tnew, Q_DIM)
    x, w_out_b = _gemm(o.astype(BF16), w_out, layer, res=x, tm=batch * tnew, emit_bf16=True)
    kw_full = jnp.concatenate([win_state, kv_w], axis=1)
    return x, kv_c, kv_s, kw_full[:, kw_full.shape[1] - win_buf:], (w_main_b, w_out_b)


def _matmul(a, w, layer, **kw):
    if layer is None:
        return _gemm(a, w, **kw), w
    return _gemm(a, w, layer, emit_bf16=True, **kw)


def _conv_layer(x, h, prev, w_in, conv_k, w_out, layer, n_seq, t, y_dtype):
    d = x.shape[1]
    tm = min(1024, n_seq * t)
    z, w_in_b = _matmul(h, w_in, layer, tm=tm)
    z = z.reshape(n_seq, t, 3 * d)
    t_pad = -(-t // 8) * 8
    z = jnp.pad(z, ((0, 0), (0, t_pad - t), (0, 0)))
    y, last = _conv_mix(z, prev, conv_k, y_dtype, t)
    y = y[:, :t].reshape(n_seq * t, d).astype(BF16)
    x, w_out_b = _matmul(y, w_out, layer, res=x, tm=tm)
    return x, last, (w_in_b, w_out_b)


def _mlp(x, g, w_up, w_down, layer):
    m = x.shape[0]
    h = _rmsnorm(x, g, BF16)
    a, w_up_b = _matmul(h, w_up, layer, act="relu2", out_dtype=BF16, tm=min(1024, m))
    x, w_down_b = _matmul(a, w_down, layer, res=x, tm=min(1024, m), tk=4096)
    return x, (w_up_b, w_down_b)


def kernel(x_prompt, x_sample, cache_kv_cmp, cache_kv_slc, state_win_kv, state_conv, page_table, norm_mix, norm_mlp, norm_final, nsa_w_in, nsa_cmp_pe, nsa_cmp_w1, nsa_cmp_w2, nsa_w_out, conv_w_in, conv_kernel, conv_w_out, mlp_w_up, mlp_w_down):
    batch, seq, d = x_prompt.shape
    dec_batch, dec_seq, _ = x_sample.shape
    depth = norm_mix.shape[0]
    win_buf = state_win_kv.shape[2]
    slopes = _alibi_slopes()
    xp = x_prompt.reshape(batch * seq, d)
    xs = x_sample.reshape(dec_batch * dec_seq, d)
    kvc_p, kvc_s, kvs_p, kvs_s, win_p, win_s, cv_p, cv_s = [], [], [], [], [], [], [], []
    w_gate = _gate_weights(nsa_w_in)
    nsa_w_in_t = jnp.swapaxes(nsa_w_in, 1, 2)
    cmp_weights = _compress_weights(nsa_cmp_pe, nsa_cmp_w1, nsa_cmp_w2)
    for i in range(depth):
        j = i // 2
        hp = _rmsnorm(xp, norm_mix[i], BF16)
        hs = _rmsnorm(xs, norm_mix[i], BF16)
        if i % 2 == 0:
            xs, a_s, b_s, w_s, (w_main_b, w_out_b) = _nsa_sample_layer(
                xs, hs, nsa_w_in_t, w_gate, cmp_weights, nsa_w_out, j, slopes, cache_kv_cmp, cache_kv_slc,
                state_win_kv[j], page_table, dec_batch, dec_seq)
            xp, a_p, b_p, w_p = _nsa_prompt_layer(xp, hp, w_main_b, w_gate, cmp_weights, w_out_b, j, slopes,
                                                  batch, seq, win_buf)
            kvc_p.append(a_p); kvc_s.append(a_s)
            kvs_p.append(b_p); kvs_s.append(b_s)
            win_p.append(w_p); win_s.append(w_s)
        else:
            zeros = jnp.zeros((batch, CONV_W - 1, d), F32)
            xs, c_s, (w_in_b, w_out_b) = _conv_layer(xs, hs, state_conv[j], conv_w_in, conv_kernel[j], conv_w_out, j,
                                                     dec_batch, dec_seq, F32)
            xp, c_p, _ = _conv_layer(xp, hp, zeros, w_in_b, conv_kernel[j], w_out_b, None, batch, seq, BF16)
            cv_p.append(c_p); cv_s.append(c_s)
        xs, (w_up_b, w_down_b) = _mlp(xs, norm_mlp[i], mlp_w_up, mlp_w_down, i)
        xp, _ = _mlp(xp, norm_mlp[i], w_up_b, w_down_b, None)
    y_prompt = _rmsnorm(xp, norm_final, F32).reshape(batch, seq, d)
    y_sample = _rmsnorm(xs, norm_final, F32).reshape(dec_batch, dec_seq, d)
    return (y_prompt, y_sample, jnp.stack(kvc_p), jnp.stack(kvc_s), jnp.stack(kvs_p), jnp.stack(kvs_s),
            jnp.stack(win_p), jnp.stack(win_s), jnp.stack(cv_p), jnp.stack(cv_s))
```

```python
import functools

import numpy as np
import jax
import jax.numpy as jnp
from jax import lax
from jax.experimental import pallas as pl
from jax.experimental.pallas import tpu as pltpu

N_HEADS = 32
HEAD_DIM = 128
N_KV = 4
GROUP = N_HEADS // N_KV
CMP_LEN = 32
CMP_STRIDE = 16
SEL_BLOCK = 64
N_SEL = 16
WINDOW = 512
CONV_W = 3
PAGE_SIZE = 128
RMS_EPS = 1e-6
NEG = -1e30
BIG = 1e30
SCALE = HEAD_DIM ** -0.5
LOG2E = 1.4426950408889634
Q_DIM = N_HEADS * HEAD_DIM
KV_DIM = 2 * N_KV * HEAD_DIM
GATE_LANES = 128

V7X_VMEM_LIMIT = 60 * 1024 * 1024
BF16 = jnp.bfloat16
F32 = jnp.float32
NT_DIMS = (((1,), (1,)), ((), ()))


def _alibi_slopes():
    i = np.arange(1, N_HEADS + 1, dtype=np.float32)
    return jnp.asarray(2.0 ** (-8.0 * i / N_HEADS), dtype=jnp.float32)


def _dot(a, b):
    return jnp.dot(a, b, preferred_element_type=F32)


def _dot_nt(a, b):
    return lax.dot_general(a, b, NT_DIMS, preferred_element_type=F32)


def _dot_split3(x, m):
    hi = x.astype(BF16)
    r1 = x - hi.astype(F32)
    mid = r1.astype(BF16)
    lo = (r1 - mid.astype(F32)).astype(BF16)
    return _dot(hi, m) + _dot(mid, m) + _dot(lo, m)


def _dot_split3_nt(m, x):
    hi = x.astype(BF16)
    r1 = x - hi.astype(F32)
    mid = r1.astype(BF16)
    lo = (r1 - mid.astype(F32)).astype(BF16)
    return _dot_nt(m, hi) + _dot_nt(m, mid) + _dot_nt(m, lo)


def _rms_kernel(x_ref, g_ref, o_ref):
    x = x_ref[...]
    ms = jnp.mean(x * x, axis=-1, keepdims=True)
    o_ref[...] = (x * lax.rsqrt(ms + RMS_EPS) * g_ref[...]).astype(o_ref.dtype)


def _rmsnorm(x, g, out_dtype):
    m, d = x.shape
    tm = min(m, 256)
    return pl.pallas_call(
        _rms_kernel,
        grid=(m // tm,),
        in_specs=[pl.BlockSpec((tm, d), lambda i: (i, 0)),
                  pl.BlockSpec((1, d), lambda i: (0, 0))],
        out_specs=pl.BlockSpec((tm, d), lambda i: (i, 0)),
        out_shape=jax.ShapeDtypeStruct((m, d), out_dtype),
        name="rmsnorm",
    )(x, g.reshape(1, d))


def _gemm_kernel(*refs, nk, act, has_res, emit_bf16, trans_w):
    a_ref, w_ref = refs[0], refs[1]
    res_ref = refs[2] if has_res else None
    o_ref = refs[3] if has_res else refs[2]
    w = w_ref[...]
    if emit_bf16:
        w = w.astype(BF16)
        refs[-1][...] = w
    dot = _dot_nt if trans_w else _dot

    def finish(acc):
        if act == "relu2":
            acc = jnp.maximum(acc, 0.0)
            acc = acc * acc
        if has_res:
            acc = acc + res_ref[...]
        o_ref[...] = acc.astype(o_ref.dtype)

    if nk == 1:
        finish(dot(a_ref[...], w))
        return

    k = pl.program_id(2)
    part = dot(a_ref[...], w)

    @pl.when(k == 0)
    def _():
        o_ref[...] = part + res_ref[...] if has_res else part

    @pl.when(k > 0)
    def _():
        o_ref[...] += part


def _gemm(a, w, layer=None, *, n=None, res=None, act=None, out_dtype=F32, tm=1024, tn=1024, tk=None,
          emit_bf16=False, trans_w=False):
    m, kdim = a.shape
    n = w.shape[-2 if trans_w else -1] if n is None else n
    tm, tn = min(tm, m), min(tn, n)
    tk = kdim if tk is None else min(tk, kdim)
    assert m % tm == 0 and n % tn == 0 and kdim % tk == 0
    nk = kdim // tk
    assert nk == 1 or (act is None and out_dtype == F32)
    assert not emit_bf16 or m == tm
    w_block = (tn, tk) if trans_w else (tk, tn)

    def w_index(i, j, k):
        return (j, k) if trans_w else (k, j)

    if layer is None:
        w_spec = pl.BlockSpec(w_block, w_index)
    else:
        w_spec = pl.BlockSpec((None,) + w_block, lambda i, j, k: (layer,) + w_index(i, j, k))
    in_specs = [pl.BlockSpec((tm, tk), lambda i, j, k: (i, k)), w_spec]
    args = [a, w]
    if res is not None:
        in_specs.append(pl.BlockSpec((tm, tn), lambda i, j, k: (i, j)))
        args.append(res)
    out_specs = pl.BlockSpec((tm, tn), lambda i, j, k: (i, j))
    out_shape = jax.ShapeDtypeStruct((m, n), out_dtype)
    if emit_bf16:
        out_specs = [out_specs, pl.BlockSpec(w_block, w_index)]
        out_shape = [out_shape, jax.ShapeDtypeStruct((n, kdim) if trans_w else (kdim, n), BF16)]
    return pl.pallas_call(
        functools.partial(_gemm_kernel, nk=nk, act=act, has_res=res is not None, emit_bf16=emit_bf16,
                          trans_w=trans_w),
        grid=(m // tm, n // tn, nk),
        in_specs=in_specs,
        out_specs=out_specs,
        out_shape=out_shape,
        compiler_params=pltpu.CompilerParams(
            dimension_semantics=("parallel", "parallel", "arbitrary"),
            vmem_limit_bytes=V7X_VMEM_LIMIT),
        name="gemm",
    )(*args)


def _compress_kernel(*refs):
    x_refs, (pe_ref, w1_ref, w2_ref, o_ref) = refs[:-4], refs[-4:]
    planes = len(x_refs)
    n_chunks = x_refs[0].shape[0] // CMP_STRIDE
    half = CMP_STRIDE * HEAD_DIM
    x = jnp.concatenate(
        [jnp.concatenate([x_ref[pl.ds(l, n_chunks, stride=CMP_STRIDE), :] for l in range(CMP_STRIDE)], axis=1)
         for x_ref in x_refs], axis=0)
    lo = _dot((x + pe_ref[0:1, :]).astype(BF16), w1_ref[0:half, :])
    hi = _dot((x + pe_ref[1:2, :]).astype(BF16), w1_ref[half:2 * half, :])
    hid = jax.nn.gelu(lo + pltpu.roll(hi, planes * n_chunks - 1, 0))
    out = _dot(hid.astype(BF16), w2_ref[...])
    for p in range(planes):
        o_ref[p] = out[p * n_chunks:(p + 1) * n_chunks]


def _compress_weights(pe, w1, w2):
    n_layers = pe.shape[0]
    half = CMP_STRIDE * HEAD_DIM
    pe2 = pe.reshape(n_layers, 2, CMP_LEN // CMP_STRIDE, half)
    w1f = w1.reshape(n_layers, 2, CMP_LEN * HEAD_DIM, w1.shape[-1]).astype(BF16)
    return pe2, w1f, w2.astype(BF16)


def _compress(x, x_specs, grid, out_index, t, n_seq, cmp_weights, layer):
    pe2, w1f, w2b = cmp_weights
    half = CMP_STRIDE * HEAD_DIM
    hidden = w1f.shape[-1]
    n_chunks = t // CMP_STRIDE
    planes = len(x_specs)

    def weight_spec(shape):
        return pl.BlockSpec((None, None) + shape, lambda c, *_: (layer, c, 0, 0))

    return pl.pallas_call(
        _compress_kernel,
        grid=(2,) + grid,
        in_specs=x_specs + [weight_spec((2, half)), weight_spec((2 * half, hidden)), weight_spec((hidden, HEAD_DIM))],
        out_specs=pl.BlockSpec((None, planes, n_chunks, HEAD_DIM), lambda c, *i: (c, out_index(*i), 0, 0)),
        out_shape=jax.ShapeDtypeStruct((2, n_seq, n_chunks, HEAD_DIM), F32),
        compiler_params=pltpu.CompilerParams(vmem_limit_bytes=V7X_VMEM_LIMIT),
        name="compress",
    )(*([x] * planes), pe2, w1f, w2b)


def _compress_prompt(z, batch, seq, cmp_weights, layer):
    kcol = Q_DIM // HEAD_DIM

    def plane_spec(g):
        return pl.BlockSpec((seq, HEAD_DIM), lambda c, b: (b, kcol + c * N_KV + g))

    return _compress(z, [plane_spec(g) for g in range(N_KV)], (batch,), lambda b: b, seq, batch * N_KV,
                     cmp_weights, layer)


def _compress_planes(x, cmp_weights, layer):
    batch, _, t, _ = x.shape
    x_spec = pl.BlockSpec((None, None, t, HEAD_DIM), lambda c, b, g: (b, c * N_KV + g, 0, 0))
    return _compress(x, [x_spec], (batch, N_KV), lambda b, g: b * N_KV + g, t, batch * N_KV, cmp_weights, layer)


COMPRESS_PAGES_PER_STEP = 8


def _compress_pages_kernel(pt_ref, *refs):
    page_refs, (pe_ref, w1_ref, w2_ref, o_ref) = refs[:-4], refs[-4:]
    half = CMP_STRIDE * HEAD_DIM
    cpp = PAGE_SIZE // CMP_STRIDE
    rows_g = len(page_refs) * cpp
    n_out = rows_g - cpp
    for c in range(2):
        x = jnp.concatenate(
            [jnp.concatenate(
                [jnp.concatenate([page_ref[pl.ds(l, cpp, stride=CMP_STRIDE), c, g, :] for l in range(CMP_STRIDE)],
                                 axis=1) for page_ref in page_refs], axis=0)
             for g in range(N_KV)], axis=0)
        lo = _dot((x + pe_ref[c, 0:1, :]).astype(BF16), w1_ref[c, 0:half, :])
        hi = _dot((x + pe_ref[c, 1:2, :]).astype(BF16), w1_ref[c, half:2 * half, :])
        hid = jax.nn.gelu(lo + pltpu.roll(hi, N_KV * rows_g - 1, 0))
        out = _dot(hid.astype(BF16), w2_ref[c])
        for g in range(N_KV):
            o_ref[c, g] = out[g * rows_g:g * rows_g + n_out]


def _compress_pages(pool, page_table, cmp_weights, layer):
    pe2, w1f, w2b = cmp_weights
    batch, n_pages = page_table.shape
    pps = COMPRESS_PAGES_PER_STEP
    assert n_pages % pps == 0
    cpp = PAGE_SIZE // CMP_STRIDE
    half = CMP_STRIDE * HEAD_DIM
    hidden = w1f.shape[-1]

    def page_spec(k):
        return pl.BlockSpec((None, None, PAGE_SIZE, 2, N_KV, HEAD_DIM),
                            lambda b, p, pt: (layer, pt[b, jnp.minimum(p * pps + k, n_pages - 1)], 0, 0, 0, 0))

    def weight_spec(shape):
        return pl.BlockSpec((None, 2) + shape, lambda b, p, pt: (layer, 0, 0, 0))

    grid_spec = pltpu.PrefetchScalarGridSpec(
        num_scalar_prefetch=1,
        grid=(batch, n_pages // pps),
        in_specs=[page_spec(k) for k in range(pps + 1)]
        + [weight_spec((2, half)), weight_spec((2 * half, hidden)), weight_spec((hidden, HEAD_DIM))],
        out_specs=pl.BlockSpec((2, N_KV, pps * cpp, HEAD_DIM), lambda b, p, pt: (0, b, p, 0)))
    return pl.pallas_call(
        _compress_pages_kernel,
        grid_spec=grid_spec,
        out_shape=jax.ShapeDtypeStruct((2, batch * N_KV, n_pages * cpp, HEAD_DIM), F32),
        compiler_params=pltpu.CompilerParams(vmem_limit_bytes=V7X_VMEM_LIMIT),
        name="compress_pages",
    )(page_table, *([pool] * (pps + 1)), pe2, w1f, w2b)


def _softmax_rows(s, mask):
    s = jnp.where(mask, s, NEG)
    m = jnp.max(s, axis=-1, keepdims=True)
    e = jnp.exp(s - m)
    return jnp.where(mask, e / jnp.sum(e, axis=-1, keepdims=True), 0.0)


def _select_blocks(imp, blk_t, n_s, k_sel):
    lanes = imp.shape[-1]
    s_idx = lax.broadcasted_iota(jnp.int32, (1, lanes), 1)
    forced = (s_idx == 0) | (s_idx == blk_t) | (s_idx == blk_t - 1)
    imp = jnp.where(s_idx <= blk_t, jnp.where(forced, BIG, imp), NEG)
    imp = jnp.where(s_idx < n_s, imp, -jnp.inf)
    s_f = s_idx.astype(F32)
    sel = jnp.zeros(imp.shape, F32)
    for _ in range(k_sel):
        top = jnp.max(imp, axis=-1, keepdims=True)
        first = jnp.min(jnp.where(imp == top, s_f, float(lanes)), axis=-1, keepdims=True)
        hit = s_f == first
        sel = jnp.where(hit, 1.0, sel)
        imp = jnp.where(hit, -jnp.inf, imp)
    return sel


def _select_blocks_t(imp_t, blk_t, n_s, k_sel):
    rows = imp_t.shape[0]
    s_idx = lax.broadcasted_iota(jnp.int32, (rows, 1), 0)
    forced = (s_idx == 0) | (s_idx == blk_t) | (s_idx == blk_t - 1)
    imp = jnp.where(s_idx <= blk_t, jnp.where(forced, BIG, imp_t), NEG)
    imp = jnp.where(s_idx < n_s, imp, -jnp.inf)
    rank = jnp.zeros(imp.shape, F32)
    for other in range(n_s):
        row = imp[other:other + 1, :]
        wins_tie = jnp.where(s_idx > other, 1.0, 0.0)
        rank = rank + jnp.where(row > imp, 1.0, jnp.where(row == imp, wins_tie, 0.0))
    return jnp.where(rank < k_sel, 1.0, 0.0)


def _overlap_matrix(n_c, n_s, rows, cols):
    ci = np.arange(rows)[:, None]
    sj = np.arange(cols)[None, :]
    ov = ((ci * CMP_STRIDE <= sj * SEL_BLOCK + SEL_BLOCK - 1)
          & (ci * CMP_STRIDE + CMP_LEN - 1 >= sj * SEL_BLOCK)
          & (ci < n_c) & (sj < n_s))
    return jnp.asarray(ov, dtype=BF16)


def _nsa_prompt_kernel(sl_ref, q_ref, ks_ref, vs_ref, kw_ref, vw_ref, kc_ref, vc_ref, gz_ref, ovt_ref,
                       o_ref, qs_sc, p_sc, oc_sc, m_sc, acc_sc, *, seq, tq, n_c, n_s, k_sel, wslab):
    g = pl.program_id(1)
    qi = pl.program_id(2)
    t0 = qi * tq
    tpos = t0 + lax.broadcasted_iota(jnp.int32, (tq, 1), 0)
    q_scaled = q_ref[...] * (SCALE * LOG2E)
    for r in range(GROUP):
        qs_sc[r * tq:(r + 1) * tq, 0:HEAD_DIM] = q_scaled[:, r * HEAD_DIM:(r + 1) * HEAD_DIM].astype(BF16)
    slopes2 = [sl_ref[g * GROUP + r] * LOG2E for r in range(GROUP)]
    ones = jnp.ones((tq, HEAD_DIM), BF16)

    def head(x, r):
        return x[r * tq:(r + 1) * tq]

    lanes_c = kc_ref.shape[0]
    n_idx = lax.broadcasted_iota(jnp.int32, (1, lanes_c), 1)
    cpos = n_idx * CMP_STRIDE + (CMP_LEN - 1)
    mask_c = (n_idx < n_c) & (tpos >= cpos)
    cpos_f = cpos.astype(F32)
    s_all = _dot_nt(qs_sc[:, 0:HEAD_DIM], kc_ref[...].astype(BF16))
    psum = jnp.zeros((tq, lanes_c), F32)
    for r in range(GROUP):
        s = jnp.where(mask_c, head(s_all, r) + slopes2[r] * cpos_f, NEG)
        e = jnp.exp2(s - jnp.max(s, axis=-1, keepdims=True))
        p = jnp.where(mask_c, e / jnp.sum(e, axis=-1, keepdims=True), 0.0)
        p_sc[r * tq:(r + 1) * tq, 0:lanes_c] = p.astype(BF16)
        psum = psum + p
    oc_sc[...] = _dot(p_sc[:, 0:lanes_c], vc_ref[...].astype(BF16))
    sel_lanes = ovt_ref.shape[0]
    s_rows = 8 * (-(-n_s // 8))
    imp_t = _dot_split3_nt(ovt_ref[...], psum)[0:s_rows]
    blk_t = (t0 + lax.broadcasted_iota(jnp.int32, (1, tq), 1)) // SEL_BLOCK
    sel_t = _select_blocks_t(imp_t, blk_t, n_s, k_sel)
    sel = jnp.concatenate([sel_t, jnp.zeros((sel_lanes - s_rows, tq), F32)], axis=0).T
    sel_bias = jnp.where(sel > 0.5, 0.0, NEG).astype(BF16)
    for r in range(GROUP):
        qs_sc[r * tq:(r + 1) * tq, HEAD_DIM:HEAD_DIM + sel_lanes] = sel_bias

    m_sc[...] = jnp.full(m_sc.shape, NEG, F32)
    acc_sc[...] = jnp.zeros(acc_sc.shape, F32)

    def key_tile(j, diagonal):
        k0 = pl.multiple_of(j * tq, tq)
        kpos = k0 + lax.broadcasted_iota(jnp.int32, (1, tq), 1)
        key_blk = (k0 + lax.broadcasted_iota(jnp.int32, (tq, sel_lanes), 0)) // SEL_BLOCK
        one_hot = jnp.where(lax.broadcasted_iota(jnp.int32, (tq, sel_lanes), 1) == key_blk, 1.0, 0.0)
        ka = jnp.concatenate([ks_ref[pl.ds(k0, tq), :].astype(BF16), one_hot.astype(BF16)], axis=1)
        va = jnp.concatenate([vs_ref[pl.ds(k0, tq), :].astype(BF16), ones], axis=1)
        kpos_f = kpos.astype(F32)
        causal = jnp.where(tpos >= kpos, 0.0, NEG) if diagonal else None
        s_all = _dot_nt(qs_sc[...], ka)
        alphas = []
        for r in range(GROUP):
            s = head(s_all, r) + slopes2[r] * kpos_f
            if diagonal:
                s = s + causal
            m_old = head(m_sc, r)
            m_new = jnp.maximum(m_old, jnp.broadcast_to(jnp.max(s, axis=-1, keepdims=True), m_old.shape))
            alphas.append(jnp.exp2(m_old - m_new))
            p_sc[r * tq:(r + 1) * tq, 0:tq] = jnp.exp2(s - jnp.concatenate([m_new, m_new], axis=1)).astype(BF16)
            m_sc[r * tq:(r + 1) * tq, :] = m_new
        pv = _dot(p_sc[:, 0:tq], va)
        for r in range(GROUP):
            alpha2 = jnp.concatenate([alphas[r], alphas[r]], axis=1)
            acc_sc[r * tq:(r + 1) * tq, :] = alpha2 * head(acc_sc, r) + head(pv, r)

    def full_tile(j, carry):
        key_tile(j, False)
        return carry

    lax.fori_loop(0, qi, full_tile, 0)
    key_tile(qi, True)

    w0 = jnp.clip(t0 - WINDOW, 0, seq - wslab)
    w0 = pl.multiple_of(w0, tq)
    kw = kw_ref[pl.ds(w0, wslab), :].astype(BF16)
    vwa = jnp.concatenate([vw_ref[pl.ds(w0, wslab), :].astype(BF16), jnp.ones((wslab, HEAD_DIM), BF16)], axis=1)
    wpos = w0 + lax.broadcasted_iota(jnp.int32, (1, wslab), 1)
    dist_w = tpos - wpos
    band = jnp.where((dist_w >= 0) & (dist_w < WINDOW), 0.0, NEG)
    wpos_f = wpos.astype(F32)
    s_all = _dot_nt(qs_sc[:, 0:HEAD_DIM], kw)
    for r in range(GROUP):
        s = head(s_all, r) + band + slopes2[r] * wpos_f
        p_sc[r * tq:(r + 1) * tq, :] = jnp.exp2(s - jnp.max(s, axis=-1, keepdims=True)).astype(BF16)
    ow_all = _dot(p_sc[...], vwa)
    gates = jax.nn.sigmoid(gz_ref[...])
    for r in range(GROUP):
        ow = head(ow_all, r)
        acc = head(acc_sc, r)
        o = (gates[:, r:r + 1] * head(oc_sc, r)
             + gates[:, GROUP + r:GROUP + r + 1] * (acc[:, :HEAD_DIM] / acc[:, HEAD_DIM:])
             + gates[:, 2 * GROUP + r:2 * GROUP + r + 1] * (ow[:, :HEAD_DIM] / ow[:, HEAD_DIM:]))
        o_ref[:, r * HEAD_DIM:(r + 1) * HEAD_DIM] = o.astype(o_ref.dtype)


def _nsa_prompt_attention(z, gz, comp, slopes, batch, seq):
    tq = min(256, seq)
    nq = seq // tq
    n_c = (seq - CMP_LEN) // CMP_STRIDE + 1
    n_s = -(-seq // SEL_BLOCK)
    k_sel = min(N_SEL, n_s)
    rows_c = comp.shape[2]
    sel_lanes = 128 * (-(-n_s // 128))
    wslab = min(WINDOW + tq, seq)
    assert wslab >= tq and wslab >= rows_c
    ov = _overlap_matrix(n_c, n_s, rows_c, sel_lanes)
    gw = Q_DIM // N_KV
    kcol = Q_DIM // HEAD_DIM

    def kv_spec(branch, c):
        base = kcol + (branch * 2 + c) * N_KV
        return pl.BlockSpec((seq, HEAD_DIM), lambda b, g, i, sl: (b, base + g))

    grid_spec = pltpu.PrefetchScalarGridSpec(
        num_scalar_prefetch=1,
        grid=(batch, N_KV, nq),
        in_specs=[pl.BlockSpec((tq, gw), lambda b, g, i, sl: (b * nq + i, g)),
                  kv_spec(1, 0), kv_spec(1, 1), kv_spec(2, 0), kv_spec(2, 1),
                  pl.BlockSpec((None, None, rows_c, HEAD_DIM), lambda b, g, i, sl: (0, b * N_KV + g, 0, 0)),
                  pl.BlockSpec((None, None, rows_c, HEAD_DIM), lambda b, g, i, sl: (1, b * N_KV + g, 0, 0)),
                  pl.BlockSpec((tq, GATE_LANES), lambda b, g, i, sl: (b * nq + i, g)),
                  pl.BlockSpec((sel_lanes, rows_c), lambda b, g, i, sl: (0, 0))],
        out_specs=pl.BlockSpec((tq, gw), lambda b, g, i, sl: (b * nq + i, g)),
        scratch_shapes=[pltpu.VMEM((GROUP * tq, HEAD_DIM + sel_lanes), BF16),
                        pltpu.VMEM((GROUP * tq, wslab), BF16),
                        pltpu.VMEM((GROUP * tq, HEAD_DIM), F32),
                        pltpu.VMEM((GROUP * tq, HEAD_DIM), F32),
                        pltpu.VMEM((GROUP * tq, 2 * HEAD_DIM), F32)])
    return pl.pallas_call(
        functools.partial(_nsa_prompt_kernel, seq=seq, tq=tq, n_c=n_c, n_s=n_s, k_sel=k_sel, wslab=wslab),
        grid_spec=grid_spec,
        out_shape=jax.ShapeDtypeStruct((batch * seq, Q_DIM), BF16),
        compiler_params=pltpu.CompilerParams(
            dimension_semantics=("parallel", "parallel", "arbitrary"),
            vmem_limit_bytes=V7X_VMEM_LIMIT),
        name="nsa_prompt",
    )(slopes, z, z, z, z, z, comp, comp, gz, ov.T)


def _nsa_sample_kernel(sl_ref, q_ref, kc_ref, vc_ref, ks_ref, vs_ref, ksn_ref, vsn_ref, kw_ref, vw_ref,
                       kwn_ref, vwn_ref, gz_ref, ov_ref, ex_ref, o_ref, *, past, tnew, n_c, n_s, k_sel):
    g = pl.program_id(1)
    rows = tnew * GROUP
    row = lax.broadcasted_iota(jnp.int32, (rows, 1), 0)
    t_new = row // GROUP
    tpos = past + t_new
    slope = jnp.zeros((rows, 1), F32)
    for r in range(GROUP):
        slope = jnp.where(row % GROUP == r, sl_ref[g * GROUP + r], slope)
    q = q_ref[...].astype(BF16)
    new_lanes = ksn_ref.shape[0]
    i_new = lax.broadcasted_iota(jnp.int32, (1, new_lanes), 1)
    dist_new = t_new - i_new
    mask_new = (i_new < tnew) & (dist_new >= 0)
    dist_new_f = dist_new.astype(F32)

    lanes_c = kc_ref.shape[0]
    n_idx = lax.broadcasted_iota(jnp.int32, (1, lanes_c), 1)
    dist_c = tpos - (n_idx * CMP_STRIDE + (CMP_LEN - 1))
    mask_c = (n_idx < n_c) & (dist_c >= 0)
    s = _dot_nt(q, kc_ref[...].astype(BF16)) * SCALE - slope * dist_c.astype(F32)
    p_c = _softmax_rows(s, mask_c)
    o_c = _dot(p_c.astype(BF16), vc_ref[...].astype(BF16))
    imp = _dot_split3(p_c, ov_ref[...])
    sel_lanes = imp.shape[1]
    imp = jnp.sum(imp.reshape(tnew, GROUP, sel_lanes), axis=1, keepdims=True)
    imp = jnp.broadcast_to(imp, (tnew, GROUP, sel_lanes)).reshape(rows, sel_lanes)
    sel = _select_blocks(imp, tpos // SEL_BLOCK, n_s, k_sel)

    def two_part_attention(s_past, mask_past, v_past, s_new, v_new):
        s_past = jnp.where(mask_past, s_past, NEG)
        s_new = jnp.where(mask_new, s_new, NEG)
        m = jnp.maximum(jnp.max(s_past, axis=-1, keepdims=True), jnp.max(s_new, axis=-1, keepdims=True))
        e_past = jnp.where(mask_past, jnp.exp(s_past - m), 0.0)
        e_new = jnp.where(mask_new, jnp.exp(s_new - m), 0.0)
        denom = jnp.sum(e_past, axis=-1, keepdims=True) + jnp.sum(e_new, axis=-1, keepdims=True)
        return (_dot(e_past.astype(BF16), v_past) + _dot(e_new.astype(BF16), v_new)) / denom

    kpos = lax.broadcasted_iota(jnp.int32, (1, past), 1)
    dist_s = (tpos - kpos).astype(F32)
    mask_s = _dot(sel.astype(BF16), ex_ref[...]) > 0.5
    s_past = _dot_nt(q, ks_ref[...].astype(BF16)) * SCALE - slope * dist_s
    s_new = _dot_nt(q, ksn_ref[...].astype(BF16)) * SCALE - slope * dist_new_f
    new_blk = past // SEL_BLOCK
    sel_new = jnp.sum(jnp.where(lax.broadcasted_iota(jnp.int32, (1, sel_lanes), 1) == new_blk, sel, 0.0),
                      axis=-1, keepdims=True) > 0.5
    mask_sn = sel_new & mask_new
    s_new = jnp.where(mask_sn, s_new, NEG)
    s_past = jnp.where(mask_s, s_past, NEG)
    m = jnp.maximum(jnp.max(s_past, axis=-1, keepdims=True), jnp.max(s_new, axis=-1, keepdims=True))
    e_past = jnp.where(mask_s, jnp.exp(s_past - m), 0.0)
    e_new = jnp.where(mask_sn, jnp.exp(s_new - m), 0.0)
    denom = jnp.sum(e_past, axis=-1, keepdims=True) + jnp.sum(e_new, axis=-1, keepdims=True)
    o_s = (_dot(e_past.astype(BF16), vs_ref[...].astype(BF16))
           + _dot(e_new.astype(BF16), vsn_ref[...].astype(BF16))) / denom

    win_buf = kw_ref.shape[0]
    wpos = (past - win_buf) + lax.broadcasted_iota(jnp.int32, (1, win_buf), 1)
    dist_w = tpos - wpos
    mask_w = (dist_w >= 0) & (dist_w < WINDOW)
    s_wp = _dot_nt(q, kw_ref[...].astype(BF16)) * SCALE - slope * dist_w.astype(F32)
    s_wn = _dot_nt(q, kwn_ref[...].astype(BF16)) * SCALE - slope * dist_new_f
    o_w = two_part_attention(s_wp, mask_w, vw_ref[...].astype(BF16), s_wn, vwn_ref[...].astype(BF16))

    gates = jax.nn.sigmoid(gz_ref[...])
    o_ref[...] = gates[:, 0:1] * o_c + gates[:, 1:2] * o_s + gates[:, 2:3] * o_w


def _nsa_sample_attention(q, comp, slc, slc_new, win, win_new, gz, slopes, tnew):
    batch = q.shape[0]
    rows = q.shape[2]
    past = slc.shape[2]
    total = past + tnew
    n_c = (total - CMP_LEN) // CMP_STRIDE + 1
    n_s = -(-total // SEL_BLOCK)
    k_sel = min(N_SEL, n_s)
    rows_c = comp.shape[2]
    assert n_c <= rows_c and (n_c - 1) * CMP_STRIDE + CMP_LEN <= past
    sel_lanes = 128 * (-(-n_s // 128))
    new_lanes = slc_new.shape[2]
    win_buf = win.shape[1]
    ov = _overlap_matrix(n_c, n_s, rows_c, sel_lanes)
    ex = jnp.asarray(np.arange(sel_lanes)[:, None] == (np.arange(past)[None, :] // SEL_BLOCK), dtype=BF16)

    def bg4(shape):
        return pl.BlockSpec((None, None) + shape, lambda b, g, sl: (b, g, 0, 0))

    def bgv4(shape):
        return pl.BlockSpec((None, None) + shape, lambda b, g, sl: (b, N_KV + g, 0, 0))

    def cmp_spec(c):
        return pl.BlockSpec((None, None, rows_c, HEAD_DIM), lambda b, g, sl: (c, b * N_KV + g, 0, 0))

    grid_spec = pltpu.PrefetchScalarGridSpec(
        num_scalar_prefetch=1,
        grid=(batch, N_KV),
        in_specs=[bg4((rows, HEAD_DIM)), cmp_spec(0), cmp_spec(1),
                  bg4((past, HEAD_DIM)), bgv4((past, HEAD_DIM)),
                  bg4((new_lanes, HEAD_DIM)), bgv4((new_lanes, HEAD_DIM)),
                  pl.BlockSpec((None, win_buf, HEAD_DIM), lambda b, g, sl: (b, 0, g)),
                  pl.BlockSpec((None, win_buf, HEAD_DIM), lambda b, g, sl: (b, 0, N_KV + g)),
                  bg4((new_lanes, HEAD_DIM)), bgv4((new_lanes, HEAD_DIM)),
                  bg4((rows, GATE_LANES)),
                  pl.BlockSpec((rows_c, sel_lanes), lambda b, g, sl: (0, 0)),
                  pl.BlockSpec((sel_lanes, past), lambda b, g, sl: (0, 0))],
        out_specs=bg4((rows, HEAD_DIM)))
    return pl.pallas_call(
        functools.partial(_nsa_sample_kernel, past=past, tnew=tnew, n_c=n_c, n_s=n_s, k_sel=k_sel),
        grid_spec=grid_spec,
        out_shape=jax.ShapeDtypeStruct((batch, N_KV, rows, HEAD_DIM), F32),
        compiler_params=pltpu.CompilerParams(
            dimension_semantics=("parallel", "arbitrary"),
            vmem_limit_bytes=V7X_VMEM_LIMIT),
        name="nsa_sample",
    )(slopes, q, comp, comp, slc, slc, slc_new, slc_new, win, win, win_new, win_new, gz, ov, ex)


GATHER_PAGES_PER_STEP = 8


def _gather_kernel(pt_ref, *refs):
    page_refs, o_ref = refs[:-1], refs[-1]
    for k, page_ref in enumerate(page_refs):
        for c in range(2):
            for g in range(N_KV):
                o_ref[c * N_KV + g, k * PAGE_SIZE:(k + 1) * PAGE_SIZE, :] = page_ref[:, c, g, :]


def _gather_pages(pool, page_table, layer):
    batch, n_pages = page_table.shape
    pps = GATHER_PAGES_PER_STEP
    assert n_pages % pps == 0
    planes = 2 * N_KV

    def page_spec(k):
        return pl.BlockSpec((None, None, PAGE_SIZE, 2, N_KV, HEAD_DIM),
                            lambda b, p, pt: (layer, pt[b, p * pps + k], 0, 0, 0, 0))

    grid_spec = pltpu.PrefetchScalarGridSpec(
        num_scalar_prefetch=1,
        grid=(batch, n_pages // pps),
        in_specs=[page_spec(k) for k in range(pps)],
        out_specs=pl.BlockSpec((None, planes, pps * PAGE_SIZE, HEAD_DIM), lambda b, p, pt: (b, 0, p, 0)))
    return pl.pallas_call(
        _gather_kernel,
        grid_spec=grid_spec,
        out_shape=jax.ShapeDtypeStruct((batch, planes, n_pages * PAGE_SIZE, HEAD_DIM), pool.dtype),
        name="gather_pages",
    )(page_table, *([pool] * pps))


def _conv_kernel(zb_ref, zc_ref, zu_ref, prev_ref, ck_ref, y_ref, last_ref, *, t_real):
    v = zc_ref[...] * zu_ref[...]
    t = v.shape[0]
    row = lax.broadcasted_iota(jnp.int32, (t, 1), 0)
    v_m1 = jnp.where(row == 0, prev_ref[1:2, :], pltpu.roll(v, 1, 0))
    v_m2 = jnp.where(row == 0, prev_ref[0:1, :], jnp.where(row == 1, prev_ref[1:2, :], pltpu.roll(v, 2, 0)))
    y = ck_ref[0:1, :] * v_m2 + ck_ref[1:2, :] * v_m1 + ck_ref[2:3, :] * v
    y_ref[...] = (zb_ref[...] * y).astype(y_ref.dtype)
    last_ref[...] = v[t_real - (CONV_W - 1):t_real]


def _conv_mix(z, prev, conv_k, out_dtype, t_real):
    assert CONV_W == 3 and t_real >= CONV_W - 1
    n_seq, t, d3 = z.shape
    d = d3 // 3
    tc = 512
    nc = d // tc
    return pl.pallas_call(
        functools.partial(_conv_kernel, t_real=t_real),
        grid=(n_seq, nc),
        in_specs=[pl.BlockSpec((None, t, tc), lambda s, c: (s, 0, c)),
                  pl.BlockSpec((None, t, tc), lambda s, c: (s, 0, nc + c)),
                  pl.BlockSpec((None, t, tc), lambda s, c: (s, 0, 2 * nc + c)),
                  pl.BlockSpec((None, CONV_W - 1, tc), lambda s, c: (s, 0, c)),
                  pl.BlockSpec((CONV_W, tc), lambda s, c: (0, c))],
        out_specs=[pl.BlockSpec((None, t, tc), lambda s, c: (s, 0, c)),
                   pl.BlockSpec((None, CONV_W - 1, tc), lambda s, c: (s, 0, c))],
        out_shape=[jax.ShapeDtypeStruct((n_seq, t, d), out_dtype),
                   jax.ShapeDtypeStruct((n_seq, CONV_W - 1, d), F32)],
        compiler_params=pltpu.CompilerParams(vmem_limit_bytes=V7X_VMEM_LIMIT),
        name="conv_mix",
    )(z, z, z, prev, conv_k)


N_MAIN = Q_DIM + 3 * KV_DIM


def _gate_weights(w_in):
    n_layers, d, _ = w_in.shape
    wg = w_in[:, :, N_MAIN:].reshape(n_layers, d, 3, N_KV, GROUP).transpose(0, 1, 3, 2, 4)
    wg = wg.reshape(n_layers, d, N_KV, 3 * GROUP)
    wg = jnp.pad(wg, ((0, 0), (0, 0), (0, 0), (0, GATE_LANES - 3 * GROUP)))
    return wg.reshape(n_layers, d, N_KV * GATE_LANES).astype(BF16)


def _split_kv(z, lead):
    return tuple(z[:, Q_DIM + j * KV_DIM:Q_DIM + (j + 1) * KV_DIM].reshape(lead + (2, N_KV, HEAD_DIM))
                 for j in range(3))


def _nsa_prompt_layer(x, h, w_main, w_gate, cmp_weights, w_out, layer, slopes, batch, seq, win_buf):
    z = _gemm(h, w_main, trans_w=True)
    gz = _gemm(h, w_gate, layer)
    kv_c, kv_s, kv_w = _split_kv(z, (batch, seq))
    comp = _compress_prompt(z, batch, seq, cmp_weights, layer)
    o = _nsa_prompt_attention(z, gz, comp, slopes, batch, seq)
    x = _gemm(o, w_out, res=x)
    pad = max(win_buf - seq, 0)
    kv_w_pad = jnp.pad(kv_w, ((0, 0), (pad, 0), (0, 0), (0, 0), (0, 0)))
    return x, kv_c, kv_s, kv_w_pad[:, kv_w_pad.shape[1] - win_buf:]


def _nsa_sample_layer(x, h, w_in_t, w_gate, cmp_weights, w_out, layer, slopes, pools_c, pools_s, win_state,
                      page_table, batch, tnew):
    z, w_main_b = _gemm(h, w_in_t, layer, n=N_MAIN, tm=batch * tnew, emit_bf16=True, trans_w=True)
    gz = _gemm(h, w_gate, layer, tm=batch * tnew)
    kv_c, kv_s, kv_w = _split_kv(z, (batch, tnew))
    slc_dense = _gather_pages(pools_s, page_table, layer)
    past = slc_dense.shape[2]
    assert past % SEL_BLOCK == 0 and tnew <= SEL_BLOCK
    comp = _compress_pages(pools_c, page_table, cmp_weights, layer)

    def new_rows(kv):
        kv = kv.transpose(0, 2, 3, 1, 4).reshape(batch, 2 * N_KV, tnew, HEAD_DIM)
        return jnp.pad(kv, ((0, 0), (0, 0), (0, 128 - tnew), (0, 0)))

    q = z[:, :Q_DIM].reshape(batch, tnew, N_KV, GROUP, HEAD_DIM).transpose(0, 2, 1, 3, 4)
    q = q.reshape(batch, N_KV, tnew * GROUP, HEAD_DIM)
    gq = gz.reshape(batch, tnew, N_KV, GATE_LANES)[..., :3 * GROUP].reshape(batch, tnew, N_KV, 3, GROUP)
    gq = gq.transpose(0, 2, 1, 4, 3).reshape(batch, N_KV, tnew * GROUP, 3)
    gq = jnp.pad(gq, ((0, 0), (0, 0), (0, 0), (0, GATE_LANES - 3)))
    win_buf = win_state.shape[1]
    o = _nsa_sample_attention(q, comp, slc_dense, new_rows(kv_s),
                              win_state.reshape(batch, win_buf, KV_DIM), new_rows(kv_w), gq, slopes, tnew)
    o = o.reshape(batch, N_KV, tnew, GROUP, HEAD_DIM).transpose(0, 2, 1, 3, 4).reshape(batch * tnew, Q_DIM)
    x, w_out_b = _gemm(o.astype(BF16), w_out, layer, res=x, tm=batch * tnew, emit_bf16=True)
    kw_full = jnp.concatenate([win_state, kv_w], axis=1)
    return x, kv_c, kv_s, kw_full[:, kw_full.shape[1] - win_buf:], (w_main_b, w_out_b)


def _matmul(a, w, layer, **kw):
    if layer is None:
        return _gemm(a, w, **kw), w
    return _gemm(a, w, layer, emit_bf16=True, **kw)


def _conv_layer(x, h, prev, w_in, conv_k, w_out, layer, n_seq, t, y_dtype):
    d = x.shape[1]
    tm = min(1024, n_seq * t)
    z, w_in_b = _matmul(h, w_in, layer, tm=tm)
    z = z.reshape(n_seq, t, 3 * d)
    t_pad = -(-t // 8) * 8
    z = jnp.pad(z, ((0, 0), (0, t_pad - t), (0, 0)))
    y, last = _conv_mix(z, prev, conv_k, y_dtype, t)
    y = y[:, :t].reshape(n_seq * t, d).astype(BF16)
    x, w_out_b = _matmul(y, w_out, layer, res=x, tm=tm)
    return x, last, (w_in_b, w_out_b)


def _mlp(x, g, w_up, w_down, layer):
    m = x.shape[0]
    h = _rmsnorm(x, g, BF16)
    a, w_up_b = _matmul(h, w_up, layer, act="relu2", out_dtype=BF16, tm=min(1024, m))
    x, w_down_b = _matmul(a, w_down, layer, res=x, tm=min(1024, m), tk=4096)
    return x, (w_up_b, w_down_b)


def kernel(x_prompt, x_sample, cache_kv_cmp, cache_kv_slc, state_win_kv, state_conv, page_table, norm_mix, norm_mlp, norm_final, nsa_w_in, nsa_cmp_pe, nsa_cmp_w1, nsa_cmp_w2, nsa_w_out, conv_w_in, conv_kernel, conv_w_out, mlp_w_up, mlp_w_down):
    batch, seq, d = x_prompt.shape
    dec_batch, dec_seq, _ = x_sample.shape
    depth = norm_mix.shape[0]
    win_buf = state_win_kv.shape[2]
    slopes = _alibi_slopes()
    xp = x_prompt.reshape(batch * seq, d)
    xs = x_sample.reshape(dec_batch * dec_seq, d)
    kvc_p, kvc_s, kvs_p, kvs_s, win_p, win_s, cv_p, cv_s = [], [], [], [], [], [], [], []
    w_gate = _gate_weights(nsa_w_in)
    nsa_w_in_t = jnp.swapaxes(nsa_w_in, 1, 2)
    cmp_weights = _compress_weights(nsa_cmp_pe, nsa_cmp_w1, nsa_cmp_w2)
    for i in range(depth):
        j = i // 2
        hp = _rmsnorm(xp, norm_mix[i], BF16)
        hs = _rmsnorm(xs, norm_mix[i], BF16)
        if i % 2 == 0:
            xs, a_s, b_s, w_s, (w_main_b, w_out_b) = _nsa_sample_layer(
                xs, hs, nsa_w_in_t, w_gate, cmp_weights, nsa_w_out, j, slopes, cache_kv_cmp, cache_kv_slc,
                state_win_kv[j], page_table, dec_batch, dec_seq)
            xp, a_p, b_p, w_p = _nsa_prompt_layer(xp, hp, w_main_b, w_gate, cmp_weights, w_out_b, j, slopes,
                                                  batch, seq, win_buf)
            kvc_p.append(a_p); kvc_s.append(a_s)
            kvs_p.append(b_p); kvs_s.append(b_s)
            win_p.append(w_p); win_s.append(w_s)
        else:
            zeros = jnp.zeros((batch, CONV_W - 1, d), F32)
            xs, c_s, (w_in_b, w_out_b) = _conv_layer(xs, hs, state_conv[j], conv_w_in, conv_kernel[j], conv_w_out, j,
                                                     dec_batch, dec_seq, F32)
            xp, c_p, _ = _conv_layer(xp, hp, zeros, w_in_b, conv_kernel[j], w_out_b, None, batch, seq, BF16)
            cv_p.append(c_p); cv_s.append(c_s)
        xs, (w_up_b, w_down_b) = _mlp(xs, norm_mlp[i], mlp_w_up, mlp_w_down, i)
        xp, _ = _mlp(xp, norm_mlp[i], w_up_b, w_down_b, None)
    y_prompt = _rmsnorm(xp, norm_final, F32).reshape(batch, seq, d)
    y_sample = _rmsnorm(xs, norm_final, F32).reshape(dec_batch, dec_seq, d)
    return (y_prompt, y_sample, jnp.stack(kvc_p), jnp.stack(kvc_s), jnp.stack(kvs_p), jnp.stack(kvs_s),
            jnp.stack(win_p), jnp.stack(win_s), jnp.stack(cv_p), jnp.stack(cv_s))
```
